```python
import math
import jax
import jax.numpy as jnp
from jax import lax
import numpy as np

D_MODEL = 1024
BATCH = 16
SEQ = 2048
DEPTH = 2

GRID_W = 64
CTX_LEN = 256
N_MIXERS = 4
GROUP_W = D_MODEL // N_MIXERS
D_MIX = N_MIXERS * GROUP_W
NORM_EPS = 1e-6
ROPE_THETA = 10000.0
Q_BLOCK = 128

RW_HD = 64
RW_HEADS = GROUP_W // RW_HD
D_DECAY_LORA = 64
D_AAA_LORA = 64
D_GATE_LORA = 128
RW_LN_EPS = 64e-5
RW_SPLITS = (GROUP_W, GROUP_W, GROUP_W, D_DECAY_LORA, D_DECAY_LORA, D_AAA_LORA, D_AAA_LORA, D_GATE_LORA)
RW_IN = 3 * GROUP_W + 2 * D_DECAY_LORA + 2 * D_AAA_LORA + D_GATE_LORA

DA_HD = 32
DA_VD = 2 * DA_HD
DA_HEADS = GROUP_W // DA_VD
DA_IN = 3 * GROUP_W

GLA_DV = 64
GLA_HEADS = GROUP_W // GLA_DV
GLA_DK = GLA_DV // 2
GLA_KW = GLA_HEADS * GLA_DK
GLA_GATE_RANK = 16
GLA_TAU = 16.0
GLA_CHUNK = 64
GLA_SPLITS = (GLA_KW, GLA_KW, GROUP_W, GLA_GATE_RANK, GLA_GATE_RANK, GROUP_W)
GLA_IN = 2 * GLA_KW + 2 * GROUP_W + 2 * GLA_GATE_RANK

GQA_HD = 64
GQA_HEADS = GROUP_W // GQA_HD
GQA_KV_HEADS = 2
GQA_GROUP = GQA_HEADS // GQA_KV_HEADS
GQA_KVW = GQA_KV_HEADS * GQA_HD
GQA_SPLITS = (GROUP_W, GQA_KVW, GQA_KVW)
GQA_IN = GROUP_W + 2 * GQA_KVW

MIXER_IN = (RW_IN, DA_IN, GLA_IN, GQA_IN)
N_IN = RW_IN + DA_IN + GLA_IN + GQA_IN
D_FF = 2816

kernel_name = 'hybrid_parallel_heads_diffusion_block'


def _cumsplit(x, widths):
    return jnp.split(x, np.cumsum(widths)[:-1].tolist(), axis=-1)


def rms_norm(x, g, eps=NORM_EPS):
    xf = x.astype(jnp.float32)
    y = xf * lax.rsqrt(jnp.mean(xf * xf, axis=-1, keepdims=True) + eps)
    return (y * g.astype(jnp.float32)).astype(x.dtype)


def modulate(x, g, shift, scale):
    return rms_norm(x, g) * (1 + scale) + shift


def shift_prev(u):
    return jnp.pad(u[:, :-1], ((0, 0), (1, 0), (0, 0)))


def shift_next(u):
    return jnp.pad(u[:, 1:], ((0, 0), (0, 1), (0, 0)))


def axial_rope_tables(rows, head_dim):
    row = jnp.repeat(jnp.arange(rows, dtype=jnp.float32), GRID_W)
    col = jnp.tile(jnp.arange(GRID_W, dtype=jnp.float32), rows)
    n_freq = head_dim // 4
    inv = ROPE_THETA ** (-jnp.arange(n_freq, dtype=jnp.float32) / n_freq)
    ang = jnp.concatenate([row[:, None] * inv, col[:, None] * inv], axis=-1)
    return jnp.cos(ang), jnp.sin(ang)


def apply_rope(x, cos, sin):
    half = x.shape[-1] // 2
    shape = (1, x.shape[1]) + (1,) * (x.ndim - 3) + (half,)
    cos = cos.reshape(shape)
    sin = sin.reshape(shape)
    xf = x.astype(jnp.float32)
    x1, x2 = xf[..., :half], xf[..., half:]
    return jnp.concatenate([x1 * cos - x2 * sin, x2 * cos + x1 * sin], axis=-1).astype(x.dtype)


def sweep_query_blocks(fn, q):
    B, T = q.shape[:2]
    nb = T // Q_BLOCK
    qb = q.reshape((B, nb, Q_BLOCK) + q.shape[2:]).swapaxes(0, 1)
    ob = lax.map(fn, qb)
    return ob.swapaxes(0, 1).reshape((B, T) + ob.shape[3:])


def rwkv7_prepare(pa, mu, w0, w2, a0, a2, g2, k_k, k_a):
    B, T, _ = pa.shape
    pa = pa + mu[0] * (shift_prev(pa) - pa) + mu[1] * (shift_next(pa) - pa)
    r, k, v, wf, wb, af, ab, g = _cumsplit(pa, RW_SPLITS)
    hv = lambda t: t.reshape(B, T, RW_HEADS, RW_HD).astype(jnp.float32)
    out_gate = jax.nn.sigmoid(g) @ g2
    kk = hv(k * k_k)
    kk = kk * lax.rsqrt(jnp.sum(kk * kk, axis=-1, keepdims=True) + 1e-12)
    per_dir = []
    for d, (wd, ad) in enumerate(((wf, af), (wb, ab))):
        w_log = -jax.nn.softplus(-(w0[d] + jnp.tanh(wd) @ w2[d])) - 0.5
        decay = jnp.exp(-jnp.exp(w_log.astype(jnp.float32)))
        a = jax.nn.sigmoid(a0[d] + ad @ a2[d])
        kd = k * (1 + (a - 1) * k_a)
        per_dir.append((hv(decay), hv(kd), hv(a)))
    return hv(r), hv(v), kk, per_dir, out_gate


def rwkv7_scan(S0, r, decay, k, v, kk, a, reverse, with_outputs):
    seq = (decay, k, v, -kk, kk * a) + ((r,) if with_outputs else ())
    xs = tuple(jnp.moveaxis(t, 1, 0) for t in seq)

    def step(S, inp):
        w_t, k_t, v_t, a_t, b_t = inp[:5]
        sa = jnp.einsum('bhvk,bhk->bhv', S, a_t)
        S = S * w_t[:, :, None, :] + sa[..., None] * b_t[:, :, None, :] + v_t[..., None] * k_t[:, :, None, :]
        y = jnp.einsum('bhvk,bhk->bhv', S, inp[5]) if with_outputs else None
        return S, y

    S, ys = lax.scan(step, S0, xs, reverse=reverse)
    return S, (jnp.moveaxis(ys, 0, 1) if with_outputs else None)


def rwkv7_output(y, r, v, dirs, r_k, ln_g, ln_b, gate):
    B, T = y.shape[:2]
    mean = jnp.mean(y, axis=-1, keepdims=True)
    var = jnp.mean(jnp.square(y - mean), axis=-1, keepdims=True)
    yn = ((y - mean) * lax.rsqrt(var + RW_LN_EPS)).reshape(B, T, GROUP_W) * ln_g + ln_b
    bonus = sum(jnp.sum(r * kd * r_k, axis=-1, keepdims=True) * v for (_, kd, _) in dirs)
    return (yn + bonus.reshape(B, T, GROUP_W)) * gate


def rwkv7_mixer(pa_l, pa_c, mu, w0, w2, a0, a2, g2, k_k, k_a, r_k, ln_g, ln_b, need_ctx):
    r_l, v_l, kk_l, dirs_l, gate_l = rwkv7_prepare(pa_l, mu, w0, w2, a0, a2, g2, k_k, k_a)
    r_c, v_c, kk_c, dirs_c, gate_c = rwkv7_prepare(pa_c, mu, w0, w2, a0, a2, g2, k_k, k_a)
    B = pa_l.shape[0]
    y_l = 0.0
    y_c = 0.0
    for d in range(2):
        rev = d == 1
        dec_c, kd_c, a_c = dirs_c[d]
        S0 = jnp.zeros((B, RW_HEADS, RW_HD, RW_HD), jnp.float32)
        S_c, yc = rwkv7_scan(S0, r_c, dec_c, kd_c, v_c, kk_c, a_c, rev, need_ctx)
        dec_l, kd_l, a_l = dirs_l[d]
        _, yl = rwkv7_scan(S_c, r_l, dec_l, kd_l, v_l, kk_l, a_l, rev, True)
        y_l = y_l + yl
        if need_ctx:
            y_c = y_c + yc
    o_l = rwkv7_output(y_l, r_l, v_l, dirs_l, r_k, ln_g, ln_b, gate_l)
    o_c = rwkv7_output(y_c, r_c, v_c, dirs_c, r_k, ln_g, ln_b, gate_c) if need_ctx else None
    return o_l, o_c


def diff_attention_core(q, k, v, lam):
    s = jnp.einsum('bqhmd,bkhmd->bhmqk', q, k).astype(jnp.float32) * (DA_HD ** -0.5)
    p = jax.nn.softmax(s, axis=-1)
    att = p[:, :, 0] - lam * p[:, :, 1]
    return jnp.einsum('bhqk,bkhe->bqhe', att, v.astype(jnp.float32))


def diff_attention_mixer(pb_l, pb_c, qk_g, lam_vecs, subln_g, layer_idx, cos, sin, need_ctx):
    lam_init = 0.8 - 0.6 * math.exp(-0.3 * layer_idx)
    lv = lam_vecs.astype(jnp.float32)
    lam = jnp.exp(jnp.sum(lv[0] * lv[1])) - jnp.exp(jnp.sum(lv[2] * lv[3])) + lam_init

    def heads(p):
        B, T, _ = p.shape
        q, k, v = jnp.split(p, 3, axis=-1)
        q = rms_norm(q.reshape(B, T, DA_HEADS, 2, DA_HD), qk_g[0])
        k = rms_norm(k.reshape(B, T, DA_HEADS, 2, DA_HD), qk_g[1])
        return q, k, v.reshape(B, T, DA_HEADS, DA_VD)

    def finish(o):
        B, T = o.shape[:2]
        return (rms_norm(o, subln_g) * (1 - lam_init)).reshape(B, T, GROUP_W)

    q_l, k_l, v_l = heads(pb_l)
    q_c, k_c, v_c = heads(pb_c)
    q_l = apply_rope(q_l, cos, sin)
    k_l = apply_rope(k_l, cos, sin)
    k_all = jnp.concatenate([k_l, k_c], axis=1)
    v_all = jnp.concatenate([v_l, v_c], axis=1)
    o_l = finish(sweep_query_blocks(lambda qb: diff_attention_core(qb, k_all, v_all, lam), q_l))
    o_c = finish(diff_attention_core(q_c, k_c, v_c, lam)) if need_ctx else None
    return o_l, o_c


def gla_prepare(pc, a2, ab):
    B, T, _ = pc.shape
    q, k, v, gf, gb, r = _cumsplit(pc, GLA_SPLITS)
    hk = lambda t: t.reshape(B, T, GLA_HEADS, GLA_DK).astype(jnp.float32)
    q = hk(q) * (GLA_DK ** -0.5)
    k = hk(k)
    v = v.reshape(B, T, GLA_HEADS, GLA_DV).astype(jnp.float32)
    log_gates = [hk(jax.nn.log_sigmoid((g @ a2[d] + ab[d]).astype(jnp.float32)) / GLA_TAU)
                 for d, g in enumerate((gf, gb))]
    return q, k, v, log_gates, r


def gla_chunked(S0, k, v, lg, q=None):
    B, T, H, _ = k.shape
    nc = T // GLA_CHUNK
    chunks = lambda t: t.reshape(B, nc, GLA_CHUNK, H, t.shape[-1]).transpose(1, 0, 3, 2, 4)
    k, v, lg = chunks(k), chunks(v), chunks(lg)
    b = jnp.cumsum(lg, axis=3)
    b_end = b[:, :, :, -1:, :]
    k_end = k * jnp.exp(b_end - b)
    dec = jnp.exp(b_end[:, :, :, 0, :])
    if q is None:
        def step_state(S, inp):
            ke, vv, dd = inp
            return S * dd[..., None] + jnp.einsum('bhld,bhle->bhde', ke, vv), None
        S, _ = lax.scan(step_state, S0, (k_end, v, dec))
        return S, None
    q = chunks(q)
    q_in = q * jnp.exp(b)
    k_in = k * jnp.exp(-b)
    mask = jnp.tril(jnp.ones((GLA_CHUNK, GLA_CHUNK), dtype=bool))
    att = jnp.where(mask, jnp.einsum('cbhid,cbhjd->cbhij', q_in, k_in), 0.0)
    o_intra = jnp.einsum('cbhij,cbhje->cbhie', att, v)

    def step(S, inp):
        qi, ke, vv, dd = inp
        o = jnp.einsum('bhld,bhde->bhle', qi, S)
        return S * dd[..., None] + jnp.einsum('bhld,bhle->bhde', ke, vv), o

    S, o_inter = lax.scan(step, S0, (q_in, k_end, v, dec))
    o = (o_intra + o_inter).transpose(1, 0, 3, 2, 4).reshape(B, T, H, -1)
    return S, o


def gla_direction(S0, q, k, v, lg, reverse, with_outputs):
    flip = (lambda t: jnp.flip(t, axis=1)) if reverse else (lambda t: t)
    S, o = gla_chunked(S0, flip(k), flip(v), flip(lg), flip(q) if with_outputs else None)
    return S, (flip(o) if with_outputs else None)


def gla_mixer(pc_l, pc_c, a2, ab, norm_g, need_ctx):
    q_l, k_l, v_l, lg_l, r_l = gla_prepare(pc_l, a2, ab)
    q_c, k_c, v_c, lg_c, r_c = gla_prepare(pc_c, a2, ab)
    B = pc_l.shape[0]
    o_l = 0.0
    o_c = 0.0
    for d in range(2):
        S0 = jnp.zeros((B, GLA_HEADS, GLA_DK, GLA_DV), jnp.float32)
        S_c, oc = gla_direction(S0, q_c, k_c, v_c, lg_c[d], d == 1, need_ctx)
        _, ol = gla_direction(S_c, q_l, k_l, v_l, lg_l[d], d == 1, True)
        o_l = o_l + ol
        if need_ctx:
            o_c = o_c + oc

    def finish(o, r):
        B_, T = o.shape[:2]
        return rms_norm(o, norm_g).reshape(B_, T, GROUP_W) * jax.nn.silu(r)

    return finish(o_l, r_l), (finish(o_c, r_c) if need_ctx else None)


def gqa_core(q, k, v):
    s = jnp.einsum('bqhgd,bkhd->bhgqk', q, k).astype(jnp.float32) * (GQA_HD ** -0.5)
    p = jax.nn.softmax(s, axis=-1)
    return jnp.einsum('bhgqk,bkhd->bqhgd', p, v.astype(jnp.float32))


def gqa_mixer(pd_l, pd_c, qk_g, cos, sin, need_ctx):
    def heads(p):
        B, T, _ = p.shape
        q, k, v = _cumsplit(p, GQA_SPLITS)
        q = rms_norm(q.reshape(B, T, GQA_KV_HEADS, GQA_GROUP, GQA_HD), qk_g[0])
        k = rms_norm(k.reshape(B, T, GQA_KV_HEADS, GQA_HD), qk_g[1])
        return q, k, v.reshape(B, T, GQA_KV_HEADS, GQA_HD)

    q_l, k_l, v_l = heads(pd_l)
    q_c, k_c, v_c = heads(pd_c)
    q_l = apply_rope(q_l, cos, sin)
    k_l = apply_rope(k_l, cos, sin)
    k_all = jnp.concatenate([k_l, k_c], axis=1)
    v_all = jnp.concatenate([v_l, v_c], axis=1)
    B, T = pd_l.shape[:2]
    o_l = sweep_query_blocks(lambda qb: gqa_core(qb, k_all, v_all), q_l).reshape(B, T, GROUP_W)
    o_c = gqa_core(q_c, k_c, v_c).reshape(B, pd_c.shape[1], GROUP_W) if need_ctx else None
    return o_l, o_c


def conv_ffn(h, w_up, conv_w, conv_b, w_down):
    u, g = jnp.split(h @ w_up, 2, axis=-1)
    g = conv_w[0] * shift_prev(g) + conv_w[1] * g + conv_w[2] * shift_next(g) + conv_b
    return (jax.nn.silu(g) * u) @ w_down


def setup_inputs(seed: int = 0) -> dict:
    key = jax.random.key(seed)
    ks = iter(jax.random.split(key, 40))
    nrm = lambda shape, s: jax.random.normal(next(ks), shape, jnp.float32) * s
    L = DEPTH
    D = D_MODEL
    return {
        'x': nrm((BATCH, SEQ, D), 1.0),
        'c': nrm((BATCH, D), 1.0),
        'ctx': nrm((BATCH, CTX_LEN, D), 1.0),
        'c_ctx': nrm((D,), 1.0),
        'mod_w': nrm((L, D, 6 * D), 0.3 * D ** -0.5),
        'mod_b': nrm((L, 6 * D), 0.02),
        'norm_mix_g': 1.0 + nrm((L, D), 0.05),
        'norm_ffn_g': 1.0 + nrm((L, D), 0.05),
        'w_in': nrm((L, D, N_IN), D ** -0.5),
        'w_out': nrm((L, D_MIX, D), D_MIX ** -0.5),
        'rw_mu': jax.random.uniform(next(ks), (L, 2, RW_IN), jnp.float32, 0.0, 0.5),
        'rw_w0': nrm((L, 2, GROUP_W), 1.0) - 2.0,
        'rw_w2': nrm((L, 2, D_DECAY_LORA, GROUP_W), 0.5 * D_DECAY_LORA ** -0.5),
        'rw_a0': nrm((L, 2, GROUP_W), 0.5),
        'rw_a2': nrm((L, 2, D_AAA_LORA, GROUP_W), 0.5 * D_AAA_LORA ** -0.5),
        'rw_g2': nrm((L, D_GATE_LORA, GROUP_W), D_GATE_LORA ** -0.5),
        'rw_kk': 0.85 + nrm((L, GROUP_W), 0.05),
        'rw_ka': 1.0 + nrm((L, GROUP_W), 0.05),
        'rw_rk': nrm((L, RW_HEADS, RW_HD), 0.1),
        'rw_ln_g': 1.0 + nrm((L, GROUP_W), 0.05),
        'rw_ln_b': nrm((L, GROUP_W), 0.02),
        'da_qk_g': 1.0 + nrm((L, 2, DA_HD), 0.05),
        'da_lam': nrm((L, 4, DA_HD), 0.1),
        'da_subln_g': 1.0 + nrm((L, DA_VD), 0.05),
        'gla_a2': nrm((L, 2, GLA_GATE_RANK, GLA_KW), GLA_GATE_RANK ** -0.5),
        'gla_ab': nrm((L, 2, GLA_KW), 0.5),
        'gla_norm_g': 1.0 + nrm((L, GLA_DV), 0.05),
        'gqa_qk_g': 1.0 + nrm((L, 2, GQA_HD), 0.05),
        'ffn_w_up': nrm((L, D, 2 * D_FF), D ** -0.5),
        'ffn_conv_w': nrm((L, 3, D_FF), 0.6),
        'ffn_conv_b': nrm((L, D_FF), 0.02),
        'ffn_w_down': nrm((L, D_FF, D), D_FF ** -0.5),
    }


def reference(x, c, ctx, c_ctx, mod_w, mod_b, norm_mix_g, norm_ffn_g, w_in, w_out,
              rw_mu, rw_w0, rw_w2, rw_a0, rw_a2, rw_g2, rw_kk, rw_ka, rw_rk, rw_ln_g, rw_ln_b,
              da_qk_g, da_lam, da_subln_g, gla_a2, gla_ab, gla_norm_g, gqa_qk_g,
              ffn_w_up, ffn_conv_w, ffn_conv_b, ffn_w_down):
    n_lat = x.shape[1]
    rows = n_lat // GRID_W
    cos_da, sin_da = axial_rope_tables(rows, DA_HD)
    cos_gq, sin_gq = axial_rope_tables(rows, GQA_HD)
    xc = ctx
    for i in range(DEPTH):
        need_ctx = i < DEPTH - 1
        mod_l = [m[:, None, :] for m in jnp.split(jax.nn.silu(c) @ mod_w[i] + mod_b[i], 6, axis=-1)]
        mod_c = jnp.split(jax.nn.silu(c_ctx) @ mod_w[i] + mod_b[i], 6, axis=-1)
        h_l = modulate(x, norm_mix_g[i], mod_l[0], mod_l[1])
        h_c = modulate(xc, norm_mix_g[i], mod_c[0], mod_c[1])
        pa_l, pb_l, pc_l, pd_l = _cumsplit(h_l @ w_in[i], MIXER_IN)
        pa_c, pb_c, pc_c, pd_c = _cumsplit(h_c @ w_in[i], MIXER_IN)
        oa_l, oa_c = rwkv7_mixer(pa_l, pa_c, rw_mu[i], rw_w0[i], rw_w2[i], rw_a0[i], rw_a2[i], rw_g2[i],
                                 rw_kk[i], rw_ka[i], rw_rk[i], rw_ln_g[i], rw_ln_b[i], need_ctx)
        ob_l, ob_c = diff_attention_mixer(pb_l, pb_c, da_qk_g[i], da_lam[i], da_subln_g[i], i,
                                          cos_da, sin_da, need_ctx)
        oc_l, oc_c = gla_mixer(pc_l, pc_c, gla_a2[i], gla_ab[i], gla_norm_g[i], need_ctx)
        od_l, od_c = gqa_mixer(pd_l, pd_c, gqa_qk_g[i], cos_gq, sin_gq, need_ctx)
        o_l = jnp.concatenate([oa_l, ob_l, oc_l, od_l], axis=-1).astype(x.dtype)
        x = x + mod_l[2] * (o_l @ w_out[i])
        if need_ctx:
            o_c = jnp.concatenate([oa_c, ob_c, oc_c, od_c], axis=-1).astype(xc.dtype)
            xc = xc + mod_c[2] * (o_c @ w_out[i])
        x = x + mod_l[5] * conv_ffn(modulate(x, norm_ffn_g[i], mod_l[3], mod_l[4]),
                                    ffn_w_up[i], ffn_conv_w[i], ffn_conv_b[i], ffn_w_down[i])
        if need_ctx:
            xc = xc + mod_c[5] * conv_ffn(modulate(xc, norm_ffn_g[i], mod_c[3], mod_c[4]),
                                          ffn_w_up[i], ffn_conv_w[i], ffn_conv_b[i], ffn_w_down[i])
    return x
```

```python
import functools
import math

import jax
import jax.numpy as jnp
from jax import lax
from jax.experimental import pallas as pl
from jax.experimental.pallas import tpu as pltpu

F32 = jnp.float32
BF16 = jnp.bfloat16
HIGHEST = lax.Precision.HIGHEST

D_MODEL = 1024
GRID_W = 64
GROUP_W = 256
NORM_EPS = 1e-6
ROPE_THETA = 10000.0
RW_HD = 64
RW_IN = 1152
RW_LN_EPS = 64e-5
DA_HD = 32
DA_IN = 768
GLA_DK = 32
GLA_KW = 128
GLA_TAU = 16.0
GLA_IN = 800
GLA_IN_PAD = 896
GQA_HD = 64
GQA_KVW = 128
GQA_IN = 512
D_FF = 2816

CHUNK = 64
TILE = 256
HALO = 8
VMEM_LIMIT = 56 * 1024 * 1024


def _cparams(*sem):
    return pltpu.CompilerParams(dimension_semantics=sem, vmem_limit_bytes=VMEM_LIMIT)


def _bdot(a, b):
    return jnp.dot(a.astype(BF16), b.astype(BF16), preferred_element_type=F32)


def _bdot_nt(a, b):
    return lax.dot_general(a.astype(BF16), b.astype(BF16), (((1,), (1,)), ((), ())),
                           preferred_element_type=F32)


def _bdot_tn(a, b):
    return jnp.dot(a.astype(F32).T.astype(BF16), b.astype(BF16), preferred_element_type=F32)


def _hdot(a, b):
    return jnp.dot(a, b, precision=HIGHEST, preferred_element_type=F32)


def _iota(shape, dim):
    return lax.broadcasted_iota(jnp.int32, shape, dim)


def _div(x, n):
    return x >> (n.bit_length() - 1)


def _mod(x, n):
    return x & (n - 1)


def _sigmoid(x):
    return 1.0 / (1.0 + jnp.exp(-x))


def _log_sigmoid(x):
    return jnp.minimum(x, 0.0) - jnp.log(1.0 + jnp.exp(-jnp.abs(x)))


def _seg_matrix(width, seg, value):
    r = _div(_iota((width, width), 0), seg)
    c = _div(_iota((width, width), 1), seg)
    return jnp.where(r == c, value, 0.0).astype(F32)


def _modulate(x, g, shift, scale):
    y = x * lax.rsqrt(jnp.mean(x * x, axis=-1, keepdims=True) + NORM_EPS)
    return (y * g) * (1.0 + scale) + shift


def _shift_rows(x, hp_row, hn_row):
    n = x.shape[0]
    rows = _iota(x.shape, 0)
    prev = jnp.where(rows == 0, hp_row, pltpu.roll(x, 1, 0))
    nxt = jnp.where(rows == n - 1, hn_row, pltpu.roll(x, n - 1, 0))
    return prev, nxt


def _tile_type(i, nct):
    return jnp.where(i >= nct, 1, 0)


def _seq_edges(i, nct, nt):
    first = jnp.logical_or(i == 0, i == nct)
    last = jnp.logical_or(i == nct - 1, i == nt - 1)
    return first, last


def _mod_kernel(c_ref, w_ref, b_ref, o_ref):
    cc = c_ref[...]
    s = cc * _sigmoid(cc)
    o_ref[0] = _bdot(s, w_ref[0]) + b_ref[0]


def _modulation(cc, mod_w, mod_b):
    depth, d, n = mod_w.shape
    tn = 1536
    rows = cc.shape[0]
    return pl.pallas_call(
        _mod_kernel,
        grid=(depth, n // tn),
        in_specs=[pl.BlockSpec((rows, d), lambda l, j: (0, 0)),
                  pl.BlockSpec((1, d, tn), lambda l, j: (l, 0, j)),
                  pl.BlockSpec((1, 1, tn), lambda l, j: (l, 0, j))],
        out_specs=pl.BlockSpec((1, rows, tn), lambda l, j: (l, 0, j)),
        out_shape=jax.ShapeDtypeStruct((depth, rows, n), F32),
        compiler_params=_cparams("parallel", "parallel"),
    )(cc, mod_w, mod_b.reshape(depth, 1, n))


def _inproj_kernel(x_ref, mod_ref, g_ref, wa, wb, wc, wd, pa, pb, pc, pd):
    m = mod_ref[0, 0]
    h = _modulate(x_ref[0], g_ref[...], m[0:1], m[1:2]).astype(BF16)
    pa[0] = jnp.dot(h, wa[...], preferred_element_type=F32)
    pb[0] = jnp.dot(h, wb[...], preferred_element_type=F32)
    pc[0] = jnp.dot(h, wc[...], preferred_element_type=F32)
    pd[0] = jnp.dot(h, wd[...], preferred_element_type=F32)


def _in_projection(xa, modtab, g, wa, wb, wc, wd, nct):
    b, nt_rows, d = xa.shape
    nt = nt_rows // TILE
    widths = (RW_IN, DA_IN, GLA_IN_PAD, GQA_IN)
    full = lambda w: pl.BlockSpec(w.shape, lambda bi, i: (0, 0))
    return pl.pallas_call(
        _inproj_kernel,
        grid=(b, nt),
        in_specs=[pl.BlockSpec((1, TILE, d), lambda bi, i: (bi, i, 0)),
                  pl.BlockSpec((1, 1, 6, d), lambda bi, i: (bi, _tile_type(i, nct), 0, 0)),
                  pl.BlockSpec((1, d), lambda bi, i: (0, 0)),
                  full(wa), full(wb), full(wc), full(wd)],
        out_specs=[pl.BlockSpec((1, TILE, w), lambda bi, i: (bi, i, 0)) for w in widths],
        out_shape=[jax.ShapeDtypeStruct((b, nt_rows, w), F32) for w in widths],
        compiler_params=_cparams("parallel", "parallel"),
    )(xa, modtab, g, wa, wb, wc, wd)


def _rw_prep_kernel(nct, nt, pa_ref, hp_ref, hn_ref, mu_ref, w0_ref, w2_ref, a0_ref, a2_ref,
                    g2_ref, kk_ref, ka_ref, rk_ref,
                    r_out, v_out, kk_out, lw_out, bv_out, kd_out, bonus_out, gate_out):
    i = pl.program_id(1)
    first, last = _seq_edges(i, nct, nt)
    x = pa_ref[0]
    hp = jnp.where(first, 0.0, hp_ref[0, HALO - 1:HALO, :])
    hn = jnp.where(last, 0.0, hn_ref[0, 0:1, :])
    prev, nxt = _shift_rows(x, hp, hn)
    mu = mu_ref[...]
    xm = x + mu[0:1] * (prev - x) + mu[1:2] * (nxt - x)
    r = xm[:, 0:256]
    k = xm[:, 256:512]
    v = xm[:, 512:768]
    wfb = xm[:, 768:896]
    afb = xm[:, 896:1024]
    g = xm[:, 1024:1152]
    seg = _seg_matrix(GROUP_W, RW_HD, 1.0)
    gate_out[0] = _bdot(_sigmoid(g), g2_ref[...])
    kk = k * kk_ref[...]
    kk = kk * lax.rsqrt(_hdot(kk * kk, seg) + 1e-12)
    wl = w0_ref[...] + _bdot(jnp.tanh(wfb), w2_ref[...])
    lw_out[0] = -jnp.exp(_log_sigmoid(wl) - 0.5)
    a = _sigmoid(a0_ref[...] + _bdot(afb, a2_ref[...]))
    ka = ka_ref[...]
    kd0 = k * (1.0 + (a[:, 0:256] - 1.0) * ka)
    kd1 = k * (1.0 + (a[:, 256:512] - 1.0) * ka)
    kd_out[0, :, 0:256] = kd0
    kd_out[0, :, 256:512] = kd1
    bv_out[0, :, 0:256] = kk * a[:, 0:256]
    bv_out[0, :, 256:512] = kk * a[:, 256:512]
    rk = rk_ref[...]
    bonus_out[0] = (_hdot(r * kd0 * rk, seg) + _hdot(r * kd1 * rk, seg)) * v
    r_out[0] = r
    v_out[0] = v
    kk_out[0] = kk


def _rwkv_prepare(pa, rw, nct):
    b, nt_rows, _ = pa.shape
    nt = nt_rows // TILE
    hb = TILE // HALO
    nhb = nt_rows // HALO
    full = lambda w: pl.BlockSpec(w.shape, lambda bi, i: (0,) * w.ndim)
    tile = lambda w: pl.BlockSpec((1, TILE, w), lambda bi, i: (bi, i, 0))
    consts = (rw["mu"], rw["w0"], rw["w2"], rw["a0"], rw["a2"], rw["g2"], rw["kk"], rw["ka"], rw["rk"])
    widths = (256, 256, 256, 512, 512, 512, 256, 256)
    return pl.pallas_call(
        functools.partial(_rw_prep_kernel, nct, nt),
        grid=(b, nt),
        in_specs=[tile(RW_IN),
                  pl.BlockSpec((1, HALO, RW_IN), lambda bi, i: (bi, jnp.maximum(i * hb - 1, 0), 0)),
                  pl.BlockSpec((1, HALO, RW_IN), lambda bi, i: (bi, jnp.minimum(i * hb + hb, nhb - 1), 0))]
                 + [full(w) for w in consts],
        out_specs=[tile(w) for w in widths],
        out_shape=[jax.ShapeDtypeStruct((b, nt_rows, w), F32) for w in widths],
        compiler_params=_cparams("parallel", "parallel"),
    )(pa, pa, pa, *consts)


def _big(x, width_seg):
    t = jnp.concatenate([x, x, x, x], axis=0)
    rh = _div(_iota(t.shape, 0), CHUNK)
    ch = _div(_iota(t.shape, 1), width_seg)
    return jnp.where(rh == ch, t, 0.0)


def _collapse(x):
    return x[0:CHUNK] + x[CHUNK:2 * CHUNK] + x[2 * CHUNK:3 * CHUNK] + x[3 * CHUNK:4 * CHUNK]


def _rw_direction(rev, r, v, kk, lw, bv, kd, s_ref):
    n = 4 * CHUNK
    ri = _iota((CHUNK, CHUNK), 0)
    ci = _iota((CHUNK, CHUNK), 1)
    tri = jnp.where((ci >= ri) if rev else (ci <= ri), 1.0, 0.0).astype(F32)
    c = _hdot(tri, lw)
    ctot = jnp.sum(lw, axis=0, keepdims=True)
    at = -kk * jnp.exp(c - lw)
    rt = r * jnp.exp(c)
    eneg = jnp.exp(-c)
    bt = bv * eneg
    kt = kd * eneg
    eend = jnp.exp(ctot - c)
    bh = bv * eend
    kh = kd * eend
    dend = jnp.exp(ctot)
    rt_big = _mod(_iota((n, n), 0), CHUNK)
    ct_big = _mod(_iota((n, n), 1), CHUNK)
    strict_big = (ct_big > rt_big) if rev else (ct_big < rt_big)
    rt_row = _iota((CHUNK, n), 0)
    ct_row = _mod(_iota((CHUNK, n), 1), CHUNK)
    strict_row = (ct_row > rt_row) if rev else (ct_row < rt_row)
    incl_row = (ct_row >= rt_row) if rev else (ct_row <= rt_row)
    same_head = _div(_iota((n, n), 0), CHUNK) == _div(_iota((n, n), 1), CHUNK)
    eye = _iota((n, n), 0) == _iota((n, n), 1)

    at_big = _big(at, RW_HD).astype(BF16)
    bt_big = _big(bt, RW_HD).astype(BF16)
    kt_big = _big(kt, RW_HD).astype(BF16)
    v_big = _big(v, RW_HD).astype(BF16)

    nmat = jnp.where(strict_big, _bdot_nt(at_big, bt_big), 0.0)
    a_ak = jnp.where(strict_row, _bdot_nt(at, kt_big), 0.0)
    a_rb = jnp.where(incl_row, _bdot_nt(rt, bt_big), 0.0)
    a_rk = jnp.where(incl_row, _bdot_nt(rt, kt_big), 0.0)
    tinv = jnp.where(eye, 1.0, 0.0) + nmat
    p = nmat
    for _ in range(5):
        p = _bdot(p, p)
        tinv = tinv + _bdot(tinv, p)
    w_big = _big(_bdot(a_ak, v_big), RW_HD)
    ah_big = _bdot(tinv, at_big)
    uv_big = _bdot(tinv, w_big)
    rh = rt + _bdot(a_rb, ah_big)
    yh = _bdot(a_rb, uv_big) + _bdot(a_rk, v_big)
    ah = _collapse(ah_big)
    uv = _collapse(uv_big)
    mmat = jnp.where(eye, dend, 0.0) + jnp.where(same_head, _bdot_tn(bh, ah), 0.0)
    nn = jnp.where(same_head, _bdot_tn(bh, uv) + _bdot_tn(kh, v), 0.0)
    s0 = s_ref[...]
    y = _bdot(rh, s0) + yh
    s_ref[...] = _bdot(mmat, s0) + nn
    return y


def _rw_scan_kernel(r0, v0, k0, lw0, bv0, kd0, r1, v1, k1, lw1, bv1, kd1, yf, yb, sf, sb):
    @pl.when(pl.program_id(1) == 0)
    def _():
        sf[...] = jnp.zeros_like(sf)
        sb[...] = jnp.zeros_like(sb)

    yf[0] = _rw_direction(False, r0[0], v0[0], k0[0], lw0[0], bv0[0], kd0[0], sf)
    yb[0] = _rw_direction(True, r1[0], v1[0], k1[0], lw1[0], bv1[0], kd1[0], sb)


def _scan_order(ncc, nc):
    fwd = lambda s: s
    bwd = lambda s: jnp.where(s < ncc, ncc - 1 - s, nc - 1 - (s - ncc))
    return fwd, bwd


def _rwkv_scan(r, v, kk, lw, bv, kd, ncc):
    b, nt_rows, _ = r.shape
    nc = nt_rows // CHUNK
    fwd, bwd = _scan_order(ncc, nc)

    def specs(order, d):
        one = pl.BlockSpec((1, CHUNK, 256), lambda bi, s: (bi, order(s), 0))
        two = pl.BlockSpec((1, CHUNK, 256), lambda bi, s: (bi, order(s), d))
        return [one, one, one, two, two, two]

    out = lambda order: pl.BlockSpec((1, CHUNK, 256), lambda bi, s: (bi, order(s), 0))
    return pl.pallas_call(
        _rw_scan_kernel,
        grid=(b, nc),
        in_specs=specs(fwd, 0) + specs(bwd, 1),
        out_specs=[out(fwd), out(bwd)],
        out_shape=[jax.ShapeDtypeStruct((b, nt_rows, 256), F32)] * 2,
        scratch_shapes=[pltpu.VMEM((256, 256), F32), pltpu.VMEM((256, 256), F32)],
        compiler_params=_cparams("parallel", "arbitrary"),
    )(r, v, kk, lw, bv, kd, r, v, kk, lw, bv, kd)


def _gla_direction(rev, pc, a2, ab, st_ref):
    n = 4 * CHUNK
    q = pc[:, 0:128] * (GLA_DK ** -0.5)
    k = pc[:, 128:256]
    v = pc[:, 256:512]
    lg = _log_sigmoid(_bdot(pc[:, 768:896], a2) + ab) / GLA_TAU
    ri = _iota((CHUNK, CHUNK), 0)
    ci = _iota((CHUNK, CHUNK), 1)
    tri = jnp.where((ci >= ri) if rev else (ci <= ri), 1.0, 0.0).astype(F32)
    bcum = _hdot(tri, lg)
    btot = jnp.sum(lg, axis=0, keepdims=True)
    q_in = q * jnp.exp(bcum)
    k_in = k * jnp.exp(-bcum)
    k_end = k * jnp.exp(btot - bcum)
    dec = jnp.exp(btot)
    rt_row = _iota((CHUNK, n), 0)
    ct_row = _mod(_iota((CHUNK, n), 1), CHUNK)
    incl_row = (ct_row >= rt_row) if rev else (ct_row <= rt_row)
    k_in_big = _big(k_in, GLA_DK)
    v_big = _big(v, 64)
    att = jnp.where(incl_row, _bdot_nt(q_in, k_in_big), 0.0)
    st = st_ref[...]
    o = _bdot(att, v_big) + _bdot_nt(q_in, st)
    same_head = _div(_iota((n, GLA_KW), 0), 64) == _div(_iota((n, GLA_KW), 1), GLA_DK)
    st_ref[...] = st * dec + jnp.where(same_head, _bdot_tn(v, k_end), 0.0)
    return o


def _gla_scan_kernel(pc0, pc1, a2_ref, ab_ref, of, ob, sf, sb):
    @pl.when(pl.program_id(1) == 0)
    def _():
        sf[...] = jnp.zeros_like(sf)
        sb[...] = jnp.zeros_like(sb)

    of[0] = _gla_direction(False, pc0[0], a2_ref[0], ab_ref[0], sf)
    ob[0] = _gla_direction(True, pc1[0], a2_ref[1], ab_ref[1], sb)


def _gla_scan(pc, a2pad, ab, ncc):
    b, nt_rows, w = pc.shape
    nc = nt_rows // CHUNK
    fwd, bwd = _scan_order(ncc, nc)
    blk = lambda order: pl.BlockSpec((1, CHUNK, w), lambda bi, s: (bi, order(s), 0))
    out = lambda order: pl.BlockSpec((1, CHUNK, 256), lambda bi, s: (bi, order(s), 0))
    return pl.pallas_call(
        _gla_scan_kernel,
        grid=(b, nc),
        in_specs=[blk(fwd), blk(bwd),
                  pl.BlockSpec(a2pad.shape, lambda bi, s: (0, 0, 0)),
                  pl.BlockSpec(ab.shape, lambda bi, s: (0, 0, 0))],
        out_specs=[out(fwd), out(bwd)],
        out_shape=[jax.ShapeDtypeStruct((b, nt_rows, 256), F32)] * 2,
        scratch_shapes=[pltpu.VMEM((256, GLA_KW), F32), pltpu.VMEM((256, GLA_KW), F32)],
        compiler_params=_cparams("parallel", "arbitrary"),
    )(pc, pc, a2pad, ab)


def _norm_rope(x, seg, gain, cos, sin, half):
    w = x.shape[1]
    xn = x * lax.rsqrt(_hdot(x * x, _seg_matrix(w, seg, 1.0 / seg)) + NORM_EPS) * gain
    lane = _mod(_iota(x.shape, 1), 2 * half)
    partner = jnp.where(lane < half, pltpu.roll(xn, w - half, 1), pltpu.roll(xn, half, 1))
    return xn * cos + partner * sin


def _qk_prep_kernel(pb_ref, pd_ref, cda_ref, sda_ref, cgq_ref, sgq_ref, gda_ref, ggq_ref,
                    daq, dakt, dav, gqq, gqkt, gqv):
    pb = pb_ref[0]
    pd = pd_ref[0]
    cda = cda_ref[...]
    sda = sda_ref[...]
    cgq = cgq_ref[...]
    sgq = sgq_ref[...]
    gda = gda_ref[...]
    ggq = ggq_ref[...]
    q = _norm_rope(pb[:, 0:256], DA_HD, gda[0:1], cda, sda, DA_HD // 2)
    daq[0] = (q * (DA_HD ** -0.5)).astype(BF16)
    k = _norm_rope(pb[:, 256:512], DA_HD, gda[1:2], cda, sda, DA_HD // 2)
    dakt[0] = k.T.astype(BF16)
    vda = pb[:, 512:768]
    for h in range(4):
        dav[0, h] = vda[:, 64 * h:64 * h + 64].astype(BF16)
    q = _norm_rope(pd[:, 0:256], GQA_HD, ggq[0:1], cgq, sgq, GQA_HD // 2)
    gqq[0] = (q * (GQA_HD ** -0.5)).astype(BF16)
    k = _norm_rope(pd[:, 256:384], GQA_HD, ggq[1:2, 0:128], cgq[:, 0:128], sgq[:, 0:128], GQA_HD // 2)
    gqkt[0] = k.T.astype(BF16)
    vgq = pd[:, 384:512]
    for h in range(2):
        gqv[0, h] = vgq[:, 64 * h:64 * h + 64].astype(BF16)


def _qk_prepare(pb, pd, cda, sda, cgq, sgq, gda, ggq):
    b, nt_rows, _ = pb.shape
    nt = nt_rows // TILE
    tile = lambda w: pl.BlockSpec((1, TILE, w), lambda bi, i: (bi, i, 0))
    tab = pl.BlockSpec((TILE, 256), lambda bi, i: (i, 0))
    gain = pl.BlockSpec((2, 256), lambda bi, i: (0, 0))
    kt = lambda w: pl.BlockSpec((1, w, TILE), lambda bi, i: (bi, 0, i))
    hv = lambda h: pl.BlockSpec((1, h, TILE, 64), lambda bi, i: (bi, 0, i, 0))
    return pl.pallas_call(
        _qk_prep_kernel,
        grid=(b, nt),
        in_specs=[tile(DA_IN), tile(GQA_IN), tab, tab, tab, tab, gain, gain],
        out_specs=[tile(256), kt(256), hv(4), tile(256), kt(128), hv(2)],
        out_shape=[jax.ShapeDtypeStruct((b, nt_rows, 256), BF16),
                   jax.ShapeDtypeStruct((b, 256, nt_rows), BF16),
                   jax.ShapeDtypeStruct((b, 4, nt_rows, 64), BF16),
                   jax.ShapeDtypeStruct((b, nt_rows, 256), BF16),
                   jax.ShapeDtypeStruct((b, 128, nt_rows), BF16),
                   jax.ShapeDtypeStruct((b, 2, nt_rows, 64), BF16)],
        compiler_params=_cparams("parallel", "parallel"),
    )(pb, pd, cda, sda, cgq, sgq, gda, ggq)


def _softmax_pv(q, kt, v):
    s = jnp.dot(q, kt, preferred_element_type=F32)
    e = jnp.exp(s - jnp.max(s, axis=-1, keepdims=True))
    l = jnp.sum(e, axis=-1, keepdims=True)
    return jnp.dot(e.astype(BF16), v, preferred_element_type=F32) / l


def _da_heads(lam_init, q, kt_ref, v_ref, lam_ref, g_ref, nk):
    lv = lam_ref[...]
    lam = (jnp.exp(jnp.sum(lv[0:1] * lv[1:2], axis=1, keepdims=True))
           - jnp.exp(jnp.sum(lv[2:3] * lv[3:4], axis=1, keepdims=True)) + lam_init)
    outs = []
    for h in range(4):
        v = v_ref[0, h, 0:nk, :]
        o = []
        for m in range(2):
            lo = 64 * h + 32 * m
            o.append(_softmax_pv(q[:, lo:lo + 32], kt_ref[0, lo:lo + 32, 0:nk], v))
        oh = o[0] - lam * o[1]
        oh = oh * lax.rsqrt(jnp.mean(oh * oh, axis=-1, keepdims=True) + NORM_EPS) * g_ref[...]
        outs.append(oh * (1.0 - lam_init))
    return jnp.concatenate(outs, axis=1)


def _gqa_heads(q, kt_ref, v_ref, nk):
    outs = []
    for h in range(4):
        g = h // 2
        outs.append(_softmax_pv(q[:, 64 * h:64 * h + 64], kt_ref[0, 64 * g:64 * g + 64, 0:nk],
                                v_ref[0, g, 0:nk, :]))
    return jnp.concatenate(outs, axis=1)


def _attn_kernel(heads_fn, q_tile0, nct, ctx_len, nk_all, q_ref, kt_ref, v_ref, *rest):
    *extra, o_ref = rest
    i = pl.program_id(1) + q_tile0

    @pl.when(i < nct)
    def _():
        o_ref[0] = heads_fn(q_ref[0], kt_ref, v_ref, *extra, ctx_len)

    @pl.when(i >= nct)
    def _():
        o_ref[0] = heads_fn(q_ref[0], kt_ref, v_ref, *extra, nk_all)


def _attention(heads_fn, q, kt, v, extra, q_tile0, nct):
    b, nt_rows, _ = q.shape
    nqt = nt_rows // TILE - q_tile0
    full = lambda w: pl.BlockSpec(w.shape, lambda bi, i: (0,) * w.ndim)
    return pl.pallas_call(
        functools.partial(_attn_kernel, heads_fn, q_tile0, nct, nct * TILE, nt_rows),
        grid=(b, nqt),
        in_specs=[pl.BlockSpec((1, TILE, 256), lambda bi, i: (bi, i + q_tile0, 0)),
                  pl.BlockSpec((1,) + kt.shape[1:], lambda bi, i: (bi, 0, 0)),
                  pl.BlockSpec((1,) + v.shape[1:], lambda bi, i: (bi, 0, 0, 0))]
                 + [full(w) for w in extra],
        out_specs=pl.BlockSpec((1, TILE, 256), lambda bi, i: (bi, i, 0)),
        out_shape=jax.ShapeDtypeStruct((b, nqt * TILE, 256), F32),
        compiler_params=_cparams("parallel", "parallel"),
    )(q, kt, v, *extra)


def _outproj_kernel(x_ref, mod_ref, yf, yb, bonus, gate, lng, lnb, ob, of, obk, rr, glag, od, w_ref, o_ref):
    m = mod_ref[0, 0]
    seg = _seg_matrix(GROUP_W, 64, 1.0 / 64)
    y = yf[0] + yb[0]
    mean = _hdot(y, seg)
    yc = y - mean
    var = _hdot(yc * yc, seg)
    oa = ((yc * lax.rsqrt(var + RW_LN_EPS)) * lng[...] + lnb[...] + bonus[0]) * gate[0]
    o = of[0] + obk[0]
    r = rr[0]
    oc = (o * lax.rsqrt(_hdot(o * o, seg) + NORM_EPS) * glag[...]) * (r * _sigmoid(r))
    w = w_ref[...]
    mix = (_bdot(oa, w[0:256]) + _bdot(ob[0], w[256:512]) + _bdot(oc, w[512:768]) + _bdot(od[0], w[768:1024]))
    o_ref[0] = x_ref[0] + m[2:3] * mix


def _out_projection(xa, modtab, yf, yb, bonus, gate, lng, lnb, ob, of, obk, pc, glag, od, w, t0, nct):
    b, nt_rows, d = xa.shape
    n = nt_rows // TILE - t0
    a0 = ob.shape[1] // TILE - n
    tile = lambda w_, off: pl.BlockSpec((1, TILE, w_), lambda bi, i: (bi, i + off, 0))
    row = pl.BlockSpec((1, 256), lambda bi, i: (0, 0))
    return pl.pallas_call(
        _outproj_kernel,
        grid=(b, n),
        in_specs=[tile(d, t0),
                  pl.BlockSpec((1, 1, 6, d), lambda bi, i: (bi, _tile_type(i + t0, nct), 0, 0)),
                  tile(256, t0), tile(256, t0), tile(256, t0), tile(256, t0), row, row,
                  tile(256, a0), tile(256, t0), tile(256, t0),
                  pl.BlockSpec((1, TILE, 256), lambda bi, i: (bi, i + t0, 2)), row,
                  tile(256, a0),
                  pl.BlockSpec(w.shape, lambda bi, i: (0, 0))],
        out_specs=pl.BlockSpec((1, TILE, d), lambda bi, i: (bi, i, 0)),
        out_shape=jax.ShapeDtypeStruct((b, n * TILE, d), F32),
        compiler_params=_cparams("parallel", "parallel"),
    )(xa, modtab, yf, yb, bonus, gate, lng, lnb, ob, of, obk, pc, glag, od, w)


def _ffn_kernel(t0, nct, nt, x_ref, hp_ref, hn_ref, mod_ref, g_ref, wu_ref, wg_ref, cw_ref, cb_ref, wd_ref, o_ref):
    i = pl.program_id(1) + t0
    first, last = _seq_edges(i, nct, nt)
    m = mod_ref[0, 0]
    g = g_ref[...]
    x = x_ref[0]
    h = _modulate(x, g, m[3:4], m[4:5])
    hp = jnp.where(first, 0.0, _modulate(hp_ref[0], g, m[3:4], m[4:5]))
    hn = jnp.where(last, 0.0, _modulate(hn_ref[0], g, m[3:4], m[4:5]))
    hext = jnp.concatenate([hp, h, hn], axis=0).astype(BF16)
    gext = jnp.dot(hext, wg_ref[...], preferred_element_type=F32)
    n = TILE + 2 * HALO
    prev = pltpu.roll(gext, 1, 0)[HALO:HALO + TILE]
    gg = gext[HALO:HALO + TILE]
    nxt = pltpu.roll(gext, n - 1, 0)[HALO:HALO + TILE]
    u = jnp.dot(h.astype(BF16), wu_ref[...], preferred_element_type=F32)
    cw = cw_ref[...]
    gc = cw[0:1] * prev + cw[1:2] * gg + cw[2:3] * nxt + cb_ref[...]
    act = (gc * _sigmoid(gc)) * u
    o_ref[0] = x + m[5:6] * _bdot(act, wd_ref[...])


def _ffn(xa, modtab, g, wu, wg, cw, cb, wd, t0, nct, nt):
    b, rows, d = xa.shape
    n = rows // TILE
    hb = TILE // HALO
    nhb = rows // HALO
    const = lambda w: pl.BlockSpec(w.shape, lambda bi, i: (0, 0), pipeline_mode=pl.Buffered(1))
    return pl.pallas_call(
        functools.partial(_ffn_kernel, t0, nct, nt),
        grid=(b, n),
        in_specs=[pl.BlockSpec((1, TILE, d), lambda bi, i: (bi, i, 0)),
                  pl.BlockSpec((1, HALO, d), lambda bi, i: (bi, jnp.maximum(i * hb - 1, 0), 0)),
                  pl.BlockSpec((1, HALO, d), lambda bi, i: (bi, jnp.minimum(i * hb + hb, nhb - 1), 0)),
                  pl.BlockSpec((1, 1, 6, d), lambda bi, i: (bi, _tile_type(i + t0, nct), 0, 0)),
                  pl.BlockSpec((1, d), lambda bi, i: (0, 0)),
                  const(wu), const(wg),
                  pl.BlockSpec(cw.shape, lambda bi, i: (0, 0)),
                  pl.BlockSpec(cb.shape, lambda bi, i: (0, 0)),
                  const(wd)],
        out_specs=pl.BlockSpec((1, TILE, d), lambda bi, i: (bi, i, 0)),
        out_shape=jax.ShapeDtypeStruct((b, rows, d), F32),
        compiler_params=_cparams("parallel", "parallel"),
    )(xa, xa, xa, modtab, g, wu, wg, cw, cb, wd)


def _rope_tables(ctx_len, rows, head_dim):
    row = jnp.repeat(jnp.arange(rows, dtype=F32), GRID_W)
    col = jnp.tile(jnp.arange(GRID_W, dtype=F32), rows)
    n_freq = head_dim // 4
    inv = ROPE_THETA ** (-jnp.arange(n_freq, dtype=F32) / n_freq)
    ang = jnp.concatenate([row[:, None] * inv, col[:, None] * inv], axis=-1)
    cos = jnp.concatenate([jnp.cos(ang), jnp.cos(ang)], axis=-1)
    sin = jnp.concatenate([-jnp.sin(ang), jnp.sin(ang)], axis=-1)
    reps = 256 // head_dim
    cos = jnp.tile(cos, (1, reps))
    sin = jnp.tile(sin, (1, reps))
    cos = jnp.concatenate([jnp.ones((ctx_len, 256), F32), cos], axis=0)
    sin = jnp.concatenate([jnp.zeros((ctx_len, 256), F32), sin], axis=0)
    return cos, sin


def _block_diag2(m0, m1):
    z01 = jnp.zeros((m0.shape[0], m1.shape[1]), m0.dtype)
    z10 = jnp.zeros((m1.shape[0], m0.shape[1]), m0.dtype)
    return jnp.concatenate([jnp.concatenate([m0, z01], axis=1), jnp.concatenate([z10, m1], axis=1)], axis=0)


def _gla_in_weights(w):
    d = w.shape[0]
    q, k, v, gf, gb, r = jnp.split(w, [128, 256, 512, 528, 544], axis=1)
    return jnp.concatenate([q, k, v, r, gf, gb, jnp.zeros((d, GLA_IN_PAD - GLA_IN), w.dtype)], axis=1)


def kernel(x, c, ctx, c_ctx, mod_w, mod_b, norm_mix_g, norm_ffn_g, w_in, w_out, rw_mu, rw_w0, rw_w2, rw_a0,
           rw_a2, rw_g2, rw_kk, rw_ka, rw_rk, rw_ln_g, rw_ln_b, da_qk_g, da_lam, da_subln_g, gla_a2, gla_ab,
           gla_norm_g, gqa_qk_g, ffn_w_up, ffn_conv_w, ffn_conv_b, ffn_w_down):
    b, seq, d = x.shape
    ctx_len = ctx.shape[1]
    depth = mod_w.shape[0]
    assert d == D_MODEL and seq % TILE == 0 and ctx_len % TILE == 0 and seq % GRID_W == 0
    nt_rows = ctx_len + seq
    nt = nt_rows // TILE
    nct = ctx_len // TILE
    ncc = ctx_len // CHUNK

    rows = -(-(b + 1) // 8) * 8
    cc = jnp.concatenate([c, c_ctx[None, :], jnp.zeros((rows - b - 1, d), F32)], axis=0)
    mods = _modulation(cc, mod_w, mod_b).reshape(depth, rows, 6, d)

    cda, sda = _rope_tables(ctx_len, seq // GRID_W, DA_HD)
    cgq, sgq = _rope_tables(ctx_len, seq // GRID_W, GQA_HD)

    xa = jnp.concatenate([ctx, x], axis=1)
    for i in range(depth):
        last = i == depth - 1
        t0 = nct if last else 0
        modtab = jnp.stack([jnp.broadcast_to(mods[i, b], (b, 6, d)), mods[i, :b]], axis=1)
        row = lambda v: v.reshape(1, -1)

        wi = w_in[i]
        wa = wi[:, 0:RW_IN].astype(BF16)
        wb = wi[:, RW_IN:RW_IN + DA_IN].astype(BF16)
        wc = _gla_in_weights(wi[:, RW_IN + DA_IN:RW_IN + DA_IN + GLA_IN]).astype(BF16)
        wd = wi[:, RW_IN + DA_IN + GLA_IN:].astype(BF16)
        pa, pb, pc, pd = _in_projection(xa, modtab, row(norm_mix_g[i]), wa, wb, wc, wd, nct)

        rw = dict(mu=rw_mu[i], w0=row(rw_w0[i]), w2=_block_diag2(rw_w2[i, 0], rw_w2[i, 1]),
                  a0=row(rw_a0[i]), a2=_block_diag2(rw_a2[i, 0], rw_a2[i, 1]), g2=rw_g2[i],
                  kk=row(rw_kk[i]), ka=row(rw_ka[i]), rk=row(rw_rk[i]))
        r, v, kk, lw, bv, kd, bonus, gate = _rwkv_prepare(pa, rw, nct)
        yf, yb = _rwkv_scan(r, v, kk, lw, bv, kd, ncc)

        a2pad = jnp.zeros((2, 128, GLA_KW), F32)
        a2pad = a2pad.at[0, 0:16].set(gla_a2[i, 0]).at[1, 16:32].set(gla_a2[i, 1])
        of, obk = _gla_scan(pc, a2pad, gla_ab[i].reshape(2, 1, GLA_KW), ncc)

        gda = jnp.tile(da_qk_g[i], (1, 256 // DA_HD))
        ggq = jnp.tile(gqa_qk_g[i], (1, 256 // GQA_HD))
        daq, dakt, dav, gqq, gqkt, gqv = _qk_prepare(pb, pd, cda, sda, cgq, sgq, gda, ggq)
        lam_init = 0.8 - 0.6 * math.exp(-0.3 * i)
        ob = _attention(functools.partial(_da_heads, lam_init), daq, dakt, dav,
                        (da_lam[i], row(da_subln_g[i])), t0, nct)
        od = _attention(_gqa_heads, gqq, gqkt, gqv, (), t0, nct)

        xa = _out_projection(xa, modtab, yf, yb, bonus, gate, row(rw_ln_g[i]), row(rw_ln_b[i]), ob, of, obk, pc,
                             row(jnp.tile(gla_norm_g[i], 4)), od, w_out[i].astype(BF16), t0, nct)
        wu = ffn_w_up[i]
        xa = _ffn(xa, modtab, row(norm_ffn_g[i]), wu[:, :D_FF].astype(BF16), wu[:, D_FF:].astype(BF16),
                  ffn_conv_w[i], row(ffn_conv_b[i]), ffn_w_down[i].astype(BF16), t0, nct, nt)
    return xa
```

```python
import functools
import math

import jax
import jax.numpy as jnp
from jax import lax
from jax.experimental import pallas as pl
from jax.experimental.pallas import tpu as pltpu

F32 = jnp.float32
BF16 = jnp.bfloat16
HIGHEST = lax.Precision.HIGHEST

D_MODEL = 1024
GRID_W = 64
GROUP_W = 256
NORM_EPS = 1e-6
ROPE_THETA = 10000.0
RW_HD = 64
RW_IN = 1152
RW_LN_EPS = 64e-5
DA_HD = 32
DA_IN = 768
GLA_DK = 32
GLA_KW = 128
GLA_TAU = 16.0
GLA_IN = 800
GLA_IN_PAD = 896
GQA_HD = 64
GQA_KVW = 128
GQA_IN = 512
D_FF = 2816

CHUNK = 64
TILE = 256
HALO = 8
STATE_BATCH = 8
VMEM_LIMIT = 56 * 1024 * 1024


def _cparams(*sem):
    return pltpu.CompilerParams(dimension_semantics=sem, vmem_limit_bytes=VMEM_LIMIT)


def _bdot(a, b):
    return jnp.dot(a.astype(BF16), b.astype(BF16), preferred_element_type=F32)


def _bdot_nt(a, b):
    return lax.dot_general(a.astype(BF16), b.astype(BF16), (((1,), (1,)), ((), ())),
                           preferred_element_type=F32)


def _bdot_tn(a, b):
    return jnp.dot(a.astype(F32).T.astype(BF16), b.astype(BF16), preferred_element_type=F32)


def _hdot(a, b):
    return jnp.dot(a, b, precision=HIGHEST, preferred_element_type=F32)


def _iota(shape, dim):
    return lax.broadcasted_iota(jnp.int32, shape, dim)


def _div(x, n):
    return x >> (n.bit_length() - 1)


def _mod(x, n):
    return x & (n - 1)


def _sigmoid(x):
    return 1.0 / (1.0 + jnp.exp(-x))


def _log_sigmoid(x):
    return jnp.minimum(x, 0.0) - jnp.log(1.0 + jnp.exp(-jnp.abs(x)))


def _seg_matrix(width, seg, value):
    r = _div(_iota((width, width), 0), seg)
    c = _div(_iota((width, width), 1), seg)
    return jnp.where(r == c, value, 0.0).astype(F32)


def _modulate(x, g, shift, scale):
    y = x * lax.rsqrt(jnp.mean(x * x, axis=-1, keepdims=True) + NORM_EPS)
    return (y * g) * (1.0 + scale) + shift


def _shift_rows(x, hp_row, hn_row):
    n = x.shape[0]
    rows = _iota(x.shape, 0)
    prev = jnp.where(rows == 0, hp_row, pltpu.roll(x, 1, 0))
    nxt = jnp.where(rows == n - 1, hn_row, pltpu.roll(x, n - 1, 0))
    return prev, nxt


def _tile_type(i, nct):
    return jnp.where(i >= nct, 1, 0)


def _seq_edges(i, nct, nt):
    first = jnp.logical_or(i == 0, i == nct)
    last = jnp.logical_or(i == nct - 1, i == nt - 1)
    return first, last


def _mod_kernel(c_ref, w_ref, b_ref, o_ref):
    cc = c_ref[...]
    s = cc * _sigmoid(cc)
    o_ref[0] = _bdot(s, w_ref[0]) + b_ref[0]


def _modulation(cc, mod_w, mod_b):
    depth, d, n = mod_w.shape
    tn = 1536
    rows = cc.shape[0]
    return pl.pallas_call(
        _mod_kernel,
        grid=(depth, n // tn),
        in_specs=[pl.BlockSpec((rows, d), lambda l, j: (0, 0)),
                  pl.BlockSpec((1, d, tn), lambda l, j: (l, 0, j)),
                  pl.BlockSpec((1, 1, tn), lambda l, j: (l, 0, j))],
        out_specs=pl.BlockSpec((1, rows, tn), lambda l, j: (l, 0, j)),
        out_shape=jax.ShapeDtypeStruct((depth, rows, n), F32),
        compiler_params=_cparams("parallel", "parallel"),
    )(cc, mod_w, mod_b.reshape(depth, 1, n))


def _inproj_kernel(x_ref, mod_ref, g_ref, wa, wb, wc, wd, pa, pb, pc, pd):
    m = mod_ref[0, 0]
    h = _modulate(x_ref[0], g_ref[...], m[0:1], m[1:2]).astype(BF16)
    pa[0] = jnp.dot(h, wa[...], preferred_element_type=F32)
    pb[0] = jnp.dot(h, wb[...], preferred_element_type=F32)
    pc[0] = jnp.dot(h, wc[...], preferred_element_type=F32)
    pd[0] = jnp.dot(h, wd[...], preferred_element_type=F32)


def _in_projection(xa, modtab, g, wa, wb, wc, wd, nct):
    b, nt_rows, d = xa.shape
    nt = nt_rows // TILE
    widths = (RW_IN, DA_IN, GLA_IN_PAD, GQA_IN)
    full = lambda w: pl.BlockSpec(w.shape, lambda bi, i: (0, 0))
    return pl.pallas_call(
        _inproj_kernel,
        grid=(b, nt),
        in_specs=[pl.BlockSpec((1, TILE, d), lambda bi, i: (bi, i, 0)),
                  pl.BlockSpec((1, 1, 6, d), lambda bi, i: (bi, _tile_type(i, nct), 0, 0)),
                  pl.BlockSpec((1, d), lambda bi, i: (0, 0)),
                  full(wa), full(wb), full(wc), full(wd)],
        out_specs=[pl.BlockSpec((1, TILE, w), lambda bi, i: (bi, i, 0)) for w in widths],
        out_shape=[jax.ShapeDtypeStruct((b, nt_rows, w), F32) for w in widths],
        compiler_params=_cparams("parallel", "parallel"),
    )(xa, modtab, g, wa, wb, wc, wd)


def _big(x, width_seg):
    t = jnp.concatenate([x, x, x, x], axis=0)
    rh = _div(_iota(t.shape, 0), CHUNK)
    ch = _div(_iota(t.shape, 1), width_seg)
    return jnp.where(rh == ch, t, 0.0)


def _collapse(x):
    return x[0:CHUNK] + x[CHUNK:2 * CHUNK] + x[2 * CHUNK:3 * CHUNK] + x[3 * CHUNK:4 * CHUNK]


def _chunk_masks(rev):
    n = 4 * CHUNK
    ri = _iota((CHUNK, CHUNK), 0)
    ci = _iota((CHUNK, CHUNK), 1)
    tri = jnp.where((ci >= ri) if rev else (ci <= ri), 1.0, 0.0).astype(F32)
    rt_row = _iota((CHUNK, n), 0)
    ct_row = _mod(_iota((CHUNK, n), 1), CHUNK)
    strict_row = (ct_row > rt_row) if rev else (ct_row < rt_row)
    incl_row = (ct_row >= rt_row) if rev else (ct_row <= rt_row)
    eye_row = ct_row == rt_row
    return tri, strict_row, incl_row, eye_row


def _lockstep(gens):
    results = [None] * len(gens)
    active = list(range(len(gens)))
    while active:
        for idx in list(active):
            try:
                next(gens[idx])
            except StopIteration as stop:
                results[idx] = stop.value
                active.remove(idx)
    return results


def _rw_chunk(rev, r, v, kk, lw, bv, kd):
    n = 4 * CHUNK
    tri, strict_row, incl_row, eye_row = _chunk_masks(rev)
    same_head = _div(_iota((n, n), 0), CHUNK) == _div(_iota((n, n), 1), CHUNK)
    c = _hdot(tri, lw)
    yield
    ctot = jnp.sum(lw, axis=0, keepdims=True)
    at = -kk * jnp.exp(c - lw)
    rt = r * jnp.exp(c)
    eneg = jnp.exp(-c)
    bt = bv * eneg
    kt = kd * eneg
    eend = jnp.exp(ctot - c)
    bh = bv * eend
    kh = kd * eend
    dend = jnp.exp(ctot)
    at_big = _big(at, RW_HD).astype(BF16)
    v_big = _big(v, RW_HD).astype(BF16)
    lhs = jnp.concatenate([at, rt], axis=0)
    gb = _bdot_nt(lhs, _big(bt, RW_HD))
    gk = _bdot_nt(lhs, _big(kt, RW_HD))
    yield
    nrow = jnp.where(strict_row, gb[0:CHUNK], 0.0)
    a_rb = jnp.where(incl_row, gb[CHUNK:], 0.0)
    a_ak = jnp.where(strict_row, gk[0:CHUNK], 0.0)
    a_rk = jnp.where(incl_row, gk[CHUNK:], 0.0)
    t_row = jnp.where(eye_row, 1.0, 0.0) + nrow
    q_row = _bdot(nrow, _big(nrow, CHUNK))
    wk = _bdot(jnp.concatenate([a_ak, a_rk], axis=0), v_big)
    yield
    for level in range(5):
        q_big = _big(q_row, CHUNK).astype(BF16)
        if level < 4:
            res = _bdot(jnp.concatenate([t_row, q_row], axis=0), q_big)
            t_row = t_row + res[0:CHUNK]
            q_row = res[CHUNK:]
        else:
            t_row = t_row + _bdot(t_row, q_big)
        yield
    ah = _bdot(t_row, at_big)
    uv = _bdot(t_row, _big(wk[0:CHUNK], RW_HD))
    yield
    rh = rt + _bdot(a_rb, _big(ah, RW_HD))
    yh = _bdot(a_rb, _big(uv, RW_HD)) + wk[CHUNK:]
    m_row = jnp.where(eye_row, dend, 0.0) + _collapse(jnp.where(same_head, _bdot_tn(bh, ah), 0.0))
    n_row = _collapse(jnp.where(same_head, _bdot_tn(bh, uv) + _bdot_tn(kh, v), 0.0))
    return (jnp.concatenate([m_row, rh], axis=0).astype(BF16), jnp.concatenate([n_row, yh], axis=0))


def _rw_chunk_kernel(nct, nt, pa_ref, hp_ref, hn_ref, mu_ref, w0_ref, w2_ref, a0_ref, a2_ref,
                     g2_ref, kk_ref, ka_ref, rk_ref, mrf, nyf, mrb, nyb, bonus_out, gate_out):
    i = pl.program_id(1)
    first, last = _seq_edges(i, nct, nt)
    x = pa_ref[0]
    hp = jnp.where(first, 0.0, hp_ref[0, HALO - 1:HALO, :])
    hn = jnp.where(last, 0.0, hn_ref[0, 0:1, :])
    prev, nxt = _shift_rows(x, hp, hn)
    mu = mu_ref[...]
    xm = x + mu[0:1] * (prev - x) + mu[1:2] * (nxt - x)
    r = xm[:, 0:256]
    k = xm[:, 256:512]
    v = xm[:, 512:768]
    wfb = xm[:, 768:896]
    afb = xm[:, 896:1024]
    g = xm[:, 1024:1152]
    seg = _seg_matrix(GROUP_W, RW_HD, 1.0)
    gate_out[0] = _bdot(_sigmoid(g), g2_ref[...])
    kk = k * kk_ref[...]
    kk = kk * lax.rsqrt(_hdot(kk * kk, seg) + 1e-12)
    wl = w0_ref[...] + _bdot(jnp.tanh(wfb), w2_ref[...])
    lw = -jnp.exp(_log_sigmoid(wl) - 0.5)
    a = _sigmoid(a0_ref[...] + _bdot(afb, a2_ref[...]))
    ka = ka_ref[...]
    kd = [k * (1.0 + (a[:, 256 * d:256 * d + 256] - 1.0) * ka) for d in range(2)]
    bv = [kk * a[:, 256 * d:256 * d + 256] for d in range(2)]
    rk = rk_ref[...]
    bonus_out[0] = (_hdot(r * kd[0] * rk, seg) + _hdot(r * kd[1] * rk, seg)) * v
    units = [(d, ck) for ck in range(TILE // CHUNK) for d in range(2)]
    gens = []
    for d, ck in units:
        rows = slice(ck * CHUNK, (ck + 1) * CHUNK)
        gens.append(_rw_chunk(d == 1, r[rows], v[rows], kk[rows], lw[rows, 256 * d:256 * d + 256],
                              bv[d][rows], kd[d][rows]))
    for (d, ck), (mr, ny) in zip(units, _lockstep(gens)):
        mr_out, ny_out = ((mrf, nyf), (mrb, nyb))[d]
        mr_out[0, 2 * ck * CHUNK:2 * (ck + 1) * CHUNK, :] = mr
        ny_out[0, 2 * ck * CHUNK:2 * (ck + 1) * CHUNK, :] = ny


def _rwkv_chunks(pa, rw, nct):
    b, nt_rows, _ = pa.shape
    nt = nt_rows // TILE
    hb = TILE // HALO
    nhb = nt_rows // HALO
    full = lambda w: pl.BlockSpec(w.shape, lambda bi, i: (0,) * w.ndim)
    tile = lambda rows, w: pl.BlockSpec((1, rows, w), lambda bi, i: (bi, i, 0))
    consts = (rw["mu"], rw["w0"], rw["w2"], rw["a0"], rw["a2"], rw["g2"], rw["kk"], rw["ka"], rw["rk"])
    pair = lambda dt: jax.ShapeDtypeStruct((b, 2 * nt_rows, 256), dt)
    return pl.pallas_call(
        functools.partial(_rw_chunk_kernel, nct, nt),
        grid=(b, nt),
        in_specs=[tile(TILE, RW_IN),
                  pl.BlockSpec((1, HALO, RW_IN), lambda bi, i: (bi, jnp.maximum(i * hb - 1, 0), 0)),
                  pl.BlockSpec((1, HALO, RW_IN), lambda bi, i: (bi, jnp.minimum(i * hb + hb, nhb - 1), 0))]
                 + [full(w) for w in consts],
        out_specs=[tile(2 * TILE, 256)] * 4 + [tile(TILE, 256)] * 2,
        out_shape=[pair(BF16), pair(F32), pair(BF16), pair(F32),
                   jax.ShapeDtypeStruct((b, nt_rows, 256), F32), jax.ShapeDtypeStruct((b, nt_rows, 256), F32)],
        compiler_params=_cparams("parallel", "parallel"),
    )(pa, pa, pa, *consts)


def _gla_chunk(rev, pc, a2, ab):
    n = 4 * CHUNK
    tri, _, incl_row, _ = _chunk_masks(rev)
    q = pc[:, 0:128] * (GLA_DK ** -0.5)
    k = pc[:, 128:256]
    v = pc[:, 256:512]
    lg = _log_sigmoid(_bdot(pc[:, 768:896], a2) + ab) / GLA_TAU
    yield
    bcum = _hdot(tri, lg)
    yield
    btot = jnp.sum(lg, axis=0, keepdims=True)
    q_in = q * jnp.exp(bcum)
    k_in = k * jnp.exp(-bcum)
    k_end = k * jnp.exp(btot - bcum)
    dec = jnp.exp(btot)
    att = jnp.where(incl_row, _bdot_nt(q_in, _big(k_in, GLA_DK)), 0.0)
    same_head = _div(_iota((n, GLA_KW), 0), 64) == _div(_iota((n, GLA_KW), 1), GLA_DK)
    inc = _collapse(jnp.where(same_head, _bdot_tn(v, k_end), 0.0))
    yield
    o_intra = _bdot(att, _big(v, 64))
    return q_in.astype(BF16), o_intra, inc, jnp.broadcast_to(dec, (8, GLA_KW))


def _gla_chunk_kernel(pc_ref, a2_ref, ab_ref, qf, of, nf, df, qb, ob, nb, db):
    pc = pc_ref[0]
    units = [(d, ck) for ck in range(TILE // CHUNK) for d in range(2)]
    gens = [_gla_chunk(d == 1, pc[ck * CHUNK:(ck + 1) * CHUNK], a2_ref[d], ab_ref[d]) for d, ck in units]
    for (d, ck), res in zip(units, _lockstep(gens)):
        outs = ((qf, of, nf, df), (qb, ob, nb, db))[d]
        for ref, val in zip(outs[:3], res[:3]):
            ref[0, ck * CHUNK:(ck + 1) * CHUNK, :] = val
        outs[3][0, 8 * ck:8 * ck + 8, :] = res[3]


def _gla_chunks(pc, a2pad, ab):
    b, nt_rows, w = pc.shape
    nt = nt_rows // TILE
    nc8 = nt_rows // CHUNK * 8
    tile = lambda rows, w_: pl.BlockSpec((1, rows, w_), lambda bi, i: (bi, i, 0))
    one = [tile(TILE, GLA_KW), tile(TILE, 256), tile(TILE, GLA_KW), tile(TILE // CHUNK * 8, GLA_KW)]
    shp = [jax.ShapeDtypeStruct((b, nt_rows, GLA_KW), BF16), jax.ShapeDtypeStruct((b, nt_rows, 256), F32),
           jax.ShapeDtypeStruct((b, nt_rows, GLA_KW), F32), jax.ShapeDtypeStruct((b, nc8, GLA_KW), F32)]
    return pl.pallas_call(
        _gla_chunk_kernel,
        grid=(b, nt),
        in_specs=[tile(TILE, w),
                  pl.BlockSpec(a2pad.shape, lambda bi, i: (0, 0, 0)),
                  pl.BlockSpec(ab.shape, lambda bi, i: (0, 0, 0))],
        out_specs=one + one,
        out_shape=shp + shp,
        compiler_params=_cparams("parallel", "parallel"),
    )(pc, a2pad, ab)


def _state_kernel(bg, mrf, nyf, mrb, nyb, qf, oif, gnf, dcf, qb, oib, gnb, dcb, yf, yb, of, ob, srw, sgl):
    @pl.when(pl.program_id(1) == 0)
    def _():
        srw[...] = jnp.zeros_like(srw)
        sgl[...] = jnp.zeros_like(sgl)

    for d, (mr, ny, y_out) in enumerate(((mrf, nyf, yf), (mrb, nyb, yb))):
        for j in range(bg):
            s_big = _big(srw[d, j], RW_HD).astype(BF16)
            res = jnp.dot(mr[j], s_big, preferred_element_type=F32) + ny[j]
            srw[d, j] = res[0:CHUNK]
            y_out[j] = res[CHUNK:]
    for d, (q, oi, gn, dc, o_out) in enumerate(((qf, oif, gnf, dcf, of), (qb, oib, gnb, dcb, ob))):
        for j in range(bg):
            st = sgl[d, j]
            o_out[j] = oi[j] + _bdot_nt(q[j], _big(st, GLA_DK))
            sgl[d, j] = st * dc[j, 0:1, :] + gn[j]


def _scan_order(ncc, nc):
    fwd = lambda s: s
    bwd = lambda s: jnp.where(s < ncc, ncc - 1 - s, nc - 1 - (s - ncc))
    return fwd, bwd


def _state_pass(rw_terms, gla_terms, ncc):
    mrf, nyf, mrb, nyb = rw_terms
    b = mrf.shape[0]
    bg = math.gcd(b, STATE_BATCH)
    nt_rows = mrf.shape[1] // 2
    nc = nt_rows // CHUNK
    fwd, bwd = _scan_order(ncc, nc)
    blk = lambda rows, w, order: pl.BlockSpec((bg, rows, w), lambda bi, s: (bi, order(s), 0))
    rw_in = lambda order: [blk(2 * CHUNK, 256, order)] * 2
    gla_in = lambda order: [blk(CHUNK, GLA_KW, order), blk(CHUNK, 256, order), blk(CHUNK, GLA_KW, order),
                            blk(8, GLA_KW, order)]
    out = lambda order: blk(CHUNK, 256, order)
    return pl.pallas_call(
        functools.partial(_state_kernel, bg),
        grid=(b // bg, nc),
        in_specs=rw_in(fwd) + rw_in(bwd) + gla_in(fwd) + gla_in(bwd),
        out_specs=[out(fwd), out(bwd), out(fwd), out(bwd)],
        out_shape=[jax.ShapeDtypeStruct((b, nt_rows, 256), F32)] * 4,
        scratch_shapes=[pltpu.VMEM((2, bg, CHUNK, 256), F32), pltpu.VMEM((2, bg, CHUNK, GLA_KW), F32)],
        compiler_params=_cparams("parallel", "arbitrary"),
    )(mrf, nyf, mrb, nyb, *gla_terms)


def _norm_rope(x, seg, gain, cos, sin, half):
    w = x.shape[1]
    xn = x * lax.rsqrt(_hdot(x * x, _seg_matrix(w, seg, 1.0 / seg)) + NORM_EPS) * gain
    lane = _mod(_iota(x.shape, 1), 2 * half)
    partner = jnp.where(lane < half, pltpu.roll(xn, w - half, 1), pltpu.roll(xn, half, 1))
    return xn * cos + partner * sin


def _qk_prep_kernel(pb_ref, pd_ref, cda_ref, sda_ref, cgq_ref, sgq_ref, gda_ref, ggq_ref,
                    daq, dakt, dav, gqq, gqkt, gqv):
    pb = pb_ref[0]
    pd = pd_ref[0]
    cda = cda_ref[...]
    sda = sda_ref[...]
    cgq = cgq_ref[...]
    sgq = sgq_ref[...]
    gda = gda_ref[...]
    ggq = ggq_ref[...]
    q = _norm_rope(pb[:, 0:256], DA_HD, gda[0:1], cda, sda, DA_HD // 2)
    daq[0] = (q * (DA_HD ** -0.5)).astype(BF16)
    k = _norm_rope(pb[:, 256:512], DA_HD, gda[1:2], cda, sda, DA_HD // 2)
    dakt[0] = k.T.astype(BF16)
    vda = pb[:, 512:768]
    for h in range(4):
        dav[0, h] = vda[:, 64 * h:64 * h + 64].astype(BF16)
    q = _norm_rope(pd[:, 0:256], GQA_HD, ggq[0:1], cgq, sgq, GQA_HD // 2)
    gqq[0] = (q * (GQA_HD ** -0.5)).astype(BF16)
    k = _norm_rope(pd[:, 256:384], GQA_HD, ggq[1:2, 0:128], cgq[:, 0:128], sgq[:, 0:128], GQA_HD // 2)
    gqkt[0] = k.T.astype(BF16)
    vgq = pd[:, 384:512]
    for h in range(2):
        gqv[0, h] = vgq[:, 64 * h:64 * h + 64].astype(BF16)


def _qk_prepare(pb, pd, cda, sda, cgq, sgq, gda, ggq):
    b, nt_rows, _ = pb.shape
    nt = nt_rows // TILE
    tile = lambda w: pl.BlockSpec((1, TILE, w), lambda bi, i: (bi, i, 0))
    tab = pl.BlockSpec((TILE, 256), lambda bi, i: (i, 0))
    gain = pl.BlockSpec((2, 256), lambda bi, i: (0, 0))
    kt = lambda w: pl.BlockSpec((1, w, TILE), lambda bi, i: (bi, 0, i))
    hv = lambda h: pl.BlockSpec((1, h, TILE, 64), lambda bi, i: (bi, 0, i, 0))
    return pl.pallas_call(
        _qk_prep_kernel,
        grid=(b, nt),
        in_specs=[tile(DA_IN), tile(GQA_IN), tab, tab, tab, tab, gain, gain],
        out_specs=[tile(256), kt(256), hv(4), tile(256), kt(128), hv(2)],
        out_shape=[jax.ShapeDtypeStruct((b, nt_rows, 256), BF16),
                   jax.ShapeDtypeStruct((b, 256, nt_rows), BF16),
                   jax.ShapeDtypeStruct((b, 4, nt_rows, 64), BF16),
                   jax.ShapeDtypeStruct((b, nt_rows, 256), BF16),
                   jax.ShapeDtypeStruct((b, 128, nt_rows), BF16),
                   jax.ShapeDtypeStruct((b, 2, nt_rows, 64), BF16)],
        compiler_params=_cparams("parallel", "parallel"),
    )(pb, pd, cda, sda, cgq, sgq, gda, ggq)


def _softmax_pv(q, kt, v):
    s = jnp.dot(q, kt, preferred_element_type=F32)
    e = jnp.exp(s - jnp.max(s, axis=-1, keepdims=True))
    l = jnp.sum(e, axis=-1, keepdims=True)
    return jnp.dot(e.astype(BF16), v, preferred_element_type=F32) / l


def _da_heads(lam_init, q, kt_ref, v_ref, lam_ref, g_ref, nk):
    lv = lam_ref[...]
    lam = (jnp.exp(jnp.sum(lv[0:1] * lv[1:2], axis=1, keepdims=True))
           - jnp.exp(jnp.sum(lv[2:3] * lv[3:4], axis=1, keepdims=True)) + lam_init)
    outs = []
    for h in range(4):
        v = v_ref[0, h, 0:nk, :]
        o = []
        for m in range(2):
            lo = 64 * h + 32 * m
            o.append(_softmax_pv(q[:, lo:lo + 32], kt_ref[0, lo:lo + 32, 0:nk], v))
        oh = o[0] - lam * o[1]
        oh = oh * lax.rsqrt(jnp.mean(oh * oh, axis=-1, keepdims=True) + NORM_EPS) * g_ref[...]
        outs.append(oh * (1.0 - lam_init))
    return jnp.concatenate(outs, axis=1)


def _gqa_heads(q, kt_ref, v_ref, nk):
    outs = []
    for h in range(4):
        g = h // 2
        outs.append(_softmax_pv(q[:, 64 * h:64 * h + 64], kt_ref[0, 64 * g:64 * g + 64, 0:nk],
                                v_ref[0, g, 0:nk, :]))
    return jnp.concatenate(outs, axis=1)


def _attn_kernel(heads_fn, q_tile0, nct, ctx_len, nk_all, q_ref, kt_ref, v_ref, *rest):
    *extra, o_ref = rest
    i = pl.program_id(1) + q_tile0

    @pl.when(i < nct)
    def _():
        o_ref[0] = heads_fn(q_ref[0], kt_ref, v_ref, *extra, ctx_len)

    @pl.when(i >= nct)
    def _():
        o_ref[0] = heads_fn(q_ref[0], kt_ref, v_ref, *extra, nk_all)


def _attention(heads_fn, q, kt, v, extra, q_tile0, nct):
    b, nt_rows, _ = q.shape
    nqt = nt_rows // TILE - q_tile0
    full = lambda w: pl.BlockSpec(w.shape, lambda bi, i: (0,) * w.ndim)
    return pl.pallas_call(
        functools.partial(_attn_kernel, heads_fn, q_tile0, nct, nct * TILE, nt_rows),
        grid=(b, nqt),
        in_specs=[pl.BlockSpec((1, TILE, 256), lambda bi, i: (bi, i + q_tile0, 0)),
                  pl.BlockSpec((1,) + kt.shape[1:], lambda bi, i: (bi, 0, 0)),
                  pl.BlockSpec((1,) + v.shape[1:], lambda bi, i: (bi, 0, 0, 0))]
                 + [full(w) for w in extra],
        out_specs=pl.BlockSpec((1, TILE, 256), lambda bi, i: (bi, i, 0)),
        out_shape=jax.ShapeDtypeStruct((b, nqt * TILE, 256), F32),
        compiler_params=_cparams("parallel", "parallel"),
    )(q, kt, v, *extra)


def _outproj_kernel(x_ref, mod_ref, yf, yb, bonus, gate, lng, lnb, ob, of, obk, rr, glag, od, w_ref, o_ref):
    m = mod_ref[0, 0]
    seg = _seg_matrix(GROUP_W, 64, 1.0 / 64)
    y = yf[0] + yb[0]
    mean = _hdot(y, seg)
    yc = y - mean
    var = _hdot(yc * yc, seg)
    oa = ((yc * lax.rsqrt(var + RW_LN_EPS)) * lng[...] + lnb[...] + bonus[0]) * gate[0]
    o = of[0] + obk[0]
    r = rr[0]
    oc = (o * lax.rsqrt(_hdot(o * o, seg) + NORM_EPS) * glag[...]) * (r * _sigmoid(r))
    w = w_ref[...]
    mix = (_bdot(oa, w[0:256]) + _bdot(ob[0], w[256:512]) + _bdot(oc, w[512:768]) + _bdot(od[0], w[768:1024]))
    o_ref[0] = x_ref[0] + m[2:3] * mix


def _out_projection(xa, modtab, yf, yb, bonus, gate, lng, lnb, ob, of, obk, pc, glag, od, w, t0, nct):
    b, nt_rows, d = xa.shape
    n = nt_rows // TILE - t0
    tile = lambda w_, off: pl.BlockSpec((1, TILE, w_), lambda bi, i: (bi, i + off, 0))
    row = pl.BlockSpec((1, 256), lambda bi, i: (0, 0))
    return pl.pallas_call(
        _outproj_kernel,
        grid=(b, n),
        in_specs=[tile(d, t0),
                  pl.BlockSpec((1, 1, 6, d), lambda bi, i: (bi, _tile_type(i + t0, nct), 0, 0)),
                  tile(256, t0), tile(256, t0), tile(256, t0), tile(256, t0), row, row,
                  tile(256, 0), tile(256, t0), tile(256, t0),
                  pl.BlockSpec((1, TILE, 256), lambda bi, i: (bi, i + t0, 2)), row,
                  tile(256, 0),
                  pl.BlockSpec(w.shape, lambda bi, i: (0, 0))],
        out_specs=pl.BlockSpec((1, TILE, d), lambda bi, i: (bi, i, 0)),
        out_shape=jax.ShapeDtypeStruct((b, n * TILE, d), F32),
        compiler_params=_cparams("parallel", "parallel"),
    )(xa, modtab, yf, yb, bonus, gate, lng, lnb, ob, of, obk, pc, glag, od, w)


def _ffn_kernel(t0, nct, nt, x_ref, hp_ref, hn_ref, mod_ref, g_ref, wu_ref, wg_ref, cw_ref, cb_ref, wd_ref, o_ref):
    i = pl.program_id(1) + t0
    first, last = _seq_edges(i, nct, nt)
    m = mod_ref[0, 0]
    g = g_ref[...]
    x = x_ref[0]
    h = _modulate(x, g, m[3:4], m[4:5])
    hp = jnp.where(first, 0.0, _modulate(hp_ref[0], g, m[3:4], m[4:5]))
    hn = jnp.where(last, 0.0, _modulate(hn_ref[0], g, m[3:4], m[4:5]))
    hext = jnp.concatenate([hp, h, hn], axis=0).astype(BF16)
    gext = jnp.dot(hext, wg_ref[...], preferred_element_type=F32)
    n = TILE + 2 * HALO
    prev = pltpu.roll(gext, 1, 0)[HALO:HALO + TILE]
    gg = gext[HALO:HALO + TILE]
    nxt = pltpu.roll(gext, n - 1, 0)[HALO:HALO + TILE]
    u = jnp.dot(h.astype(BF16), wu_ref[...], preferred_element_type=F32)
    cw = cw_ref[...]
    gc = cw[0:1] * prev + cw[1:2] * gg + cw[2:3] * nxt + cb_ref[...]
    act = (gc * _sigmoid(gc)) * u
    o_ref[0] = x + m[5:6] * _bdot(act, wd_ref[...])


def _ffn(xa, modtab, g, wu, wg, cw, cb, wd, t0, nct, nt):
    b, rows, d = xa.shape
    n = rows // TILE
    hb = TILE // HALO
    nhb = rows // HALO
    const = lambda w: pl.BlockSpec(w.shape, lambda bi, i: (0, 0), pipeline_mode=pl.Buffered(1))
    return pl.pallas_call(
        functools.partial(_ffn_kernel, t0, nct, nt),
        grid=(b, n),
        in_specs=[pl.BlockSpec((1, TILE, d), lambda bi, i: (bi, i, 0)),
                  pl.BlockSpec((1, HALO, d), lambda bi, i: (bi, jnp.maximum(i * hb - 1, 0), 0)),
                  pl.BlockSpec((1, HALO, d), lambda bi, i: (bi, jnp.minimum(i * hb + hb, nhb - 1), 0)),
                  pl.BlockSpec((1, 1, 6, d), lambda bi, i: (bi, _tile_type(i + t0, nct), 0, 0)),
                  pl.BlockSpec((1, d), lambda bi, i: (0, 0)),
                  const(wu), const(wg),
                  pl.BlockSpec(cw.shape, lambda bi, i: (0, 0)),
                  pl.BlockSpec(cb.shape, lambda bi, i: (0, 0)),
                  const(wd)],
        out_specs=pl.BlockSpec((1, TILE, d), lambda bi, i: (bi, i, 0)),
        out_shape=jax.ShapeDtypeStruct((b, rows, d), F32),
        compiler_params=_cparams("parallel", "parallel"),
    )(xa, xa, xa, modtab, g, wu, wg, cw, cb, wd)


def _rope_tables(ctx_len, rows, head_dim):
    row = jnp.repeat(jnp.arange(rows, dtype=F32), GRID_W)
    col = jnp.tile(jnp.arange(GRID_W, dtype=F32), rows)
    n_freq = head_dim // 4
    inv = ROPE_THETA ** (-jnp.arange(n_freq, dtype=F32) / n_freq)
    ang = jnp.concatenate([row[:, None] * inv, col[:, None] * inv], axis=-1)
    cos = jnp.concatenate([jnp.cos(ang), jnp.cos(ang)], axis=-1)
    sin = jnp.concatenate([-jnp.sin(ang), jnp.sin(ang)], axis=-1)
    reps = 256 // head_dim
    cos = jnp.tile(cos, (1, reps))
    sin = jnp.tile(sin, (1, reps))
    cos = jnp.concatenate([jnp.ones((ctx_len, 256), F32), cos], axis=0)
    sin = jnp.concatenate([jnp.zeros((ctx_len, 256), F32), sin], axis=0)
    return cos, sin


def _block_diag2(m0, m1):
    z01 = jnp.zeros((m0.shape[0], m1.shape[1]), m0.dtype)
    z10 = jnp.zeros((m1.shape[0], m0.shape[1]), m0.dtype)
    return jnp.concatenate([jnp.concatenate([m0, z01], axis=1), jnp.concatenate([z10, m1], axis=1)], axis=0)


def _gla_in_weights(w):
    d = w.shape[0]
    q, k, v, gf, gb, r = jnp.split(w, [128, 256, 512, 528, 544], axis=1)
    return jnp.concatenate([q, k, v, r, gf, gb, jnp.zeros((d, GLA_IN_PAD - GLA_IN), w.dtype)], axis=1)


def kernel(x, c, ctx, c_ctx, mod_w, mod_b, norm_mix_g, norm_ffn_g, w_in, w_out, rw_mu, rw_w0, rw_w2, rw_a0,
           rw_a2, rw_g2, rw_kk, rw_ka, rw_rk, rw_ln_g, rw_ln_b, da_qk_g, da_lam, da_subln_g, gla_a2, gla_ab,
           gla_norm_g, gqa_qk_g, ffn_w_up, ffn_conv_w, ffn_conv_b, ffn_w_down):
    b, seq, d = x.shape
    ctx_len = ctx.shape[1]
    depth = mod_w.shape[0]
    assert d == D_MODEL and seq % TILE == 0 and ctx_len % TILE == 0 and seq % GRID_W == 0
    nt_rows = ctx_len + seq
    nt = nt_rows // TILE
    nct = ctx_len // TILE
    ncc = ctx_len // CHUNK

    rows = -(-(b + 1) // 8) * 8
    cc = jnp.concatenate([c, c_ctx[None, :], jnp.zeros((rows - b - 1, d), F32)], axis=0)
    mods = _modulation(cc, mod_w, mod_b).reshape(depth, rows, 6, d)

    cda, sda = _rope_tables(ctx_len, seq // GRID_W, DA_HD)
    cgq, sgq = _rope_tables(ctx_len, seq // GRID_W, GQA_HD)

    xa = jnp.concatenate([ctx, x], axis=1)
    for i in range(depth):
        last = i == depth - 1
        t0 = nct if last else 0
        modtab = jnp.stack([jnp.broadcast_to(mods[i, b], (b, 6, d)), mods[i, :b]], axis=1)
        row = lambda v: v.reshape(1, -1)

        wi = w_in[i]
        wa = wi[:, 0:RW_IN].astype(BF16)
        wb = wi[:, RW_IN:RW_IN + DA_IN].astype(BF16)
        wc = _gla_in_weights(wi[:, RW_IN + DA_IN:RW_IN + DA_IN + GLA_IN]).astype(BF16)
        wd = wi[:, RW_IN + DA_IN + GLA_IN:].astype(BF16)
        pa, pb, pc, pd = _in_projection(xa, modtab, row(norm_mix_g[i]), wa, wb, wc, wd, nct)

        rw = dict(mu=rw_mu[i], w0=row(rw_w0[i]), w2=_block_diag2(rw_w2[i, 0], rw_w2[i, 1]),
                  a0=row(rw_a0[i]), a2=_block_diag2(rw_a2[i, 0], rw_a2[i, 1]), g2=rw_g2[i],
                  kk=row(rw_kk[i]), ka=row(rw_ka[i]), rk=row(rw_rk[i]))
        mrf, nyf, mrb, nyb, bonus, gate = _rwkv_chunks(pa, rw, nct)

        a2pad = jnp.zeros((2, 128, GLA_KW), F32)
        a2pad = a2pad.at[0, 0:16].set(gla_a2[i, 0]).at[1, 16:32].set(gla_a2[i, 1])
        gla_terms = _gla_chunks(pc, a2pad, gla_ab[i].reshape(2, 1, GLA_KW))
        yf, yb, of, obk = _state_pass((mrf, nyf, mrb, nyb), gla_terms, ncc)

        gda = jnp.tile(da_qk_g[i], (1, 256 // DA_HD))
        ggq = jnp.tile(gqa_qk_g[i], (1, 256 // GQA_HD))
        daq, dakt, dav, gqq, gqkt, gqv = _qk_prepare(pb, pd, cda, sda, cgq, sgq, gda, ggq)
        lam_init = 0.8 - 0.6 * math.exp(-0.3 * i)
        ob = _attention(functools.partial(_da_heads, lam_init), daq, dakt, dav,
                        (da_lam[i], row(da_subln_g[i])), t0, nct)
        od = _attention(_gqa_heads, gqq, gqkt, gqv, (), t0, nct)

        xa = _out_projection(xa, modtab, yf, yb, bonus, gate, row(rw_ln_g[i]), row(rw_ln_b[i]), ob, of, obk, pc,
                             row(jnp.tile(gla_norm_g[i], 4)), od, w_out[i].astype(BF16), t0, nct)
        wu = ffn_w_up[i]
        xa = _ffn(xa, modtab, row(norm_ffn_g[i]), wu[:, :D_FF].astype(BF16), wu[:, D_FF:].astype(BF16),
                  ffn_conv_w[i], row(ffn_conv_b[i]), ffn_w_down[i].astype(BF16), t0, nct, nt)
    return xa
```

```python
import functools
import math

import jax
import jax.numpy as jnp
from jax import lax
from jax.experimental import pallas as pl
from jax.experimental.pallas import tpu as pltpu

F32 = jnp.float32
BF16 = jnp.bfloat16
LOG2E = math.log2(math.e)

D_MODEL = 1024
GRID_W = 64
GROUP_W = 256
NORM_EPS = 1e-6
ROPE_THETA = 10000.0
RW_HD = 64
RW_IN = 1152
RW_LN_EPS = 64e-5
DA_HD = 32
DA_IN = 768
GLA_DK = 32
GLA_KW = 128
GLA_TAU = 16.0
GLA_IN = 800
GLA_IN_PAD = 896
GQA_HD = 64
GQA_KVW = 128
GQA_IN = 512
D_FF = 2816

CHUNK = 64
TILE = 256
HALO = 8
STATE_BATCH = 8
VMEM_LIMIT = 56 * 1024 * 1024


def _cparams(*sem):
    return pltpu.CompilerParams(dimension_semantics=sem, vmem_limit_bytes=VMEM_LIMIT)


def _bdot(a, b):
    return jnp.dot(a.astype(BF16), b.astype(BF16), preferred_element_type=F32)


def _bdot_nt(a, b):
    return lax.dot_general(a.astype(BF16), b.astype(BF16), (((1,), (1,)), ((), ())),
                           preferred_element_type=F32)


def _bdot_tn(a, b):
    return jnp.dot(a.astype(F32).T.astype(BF16), b.astype(BF16), preferred_element_type=F32)


def _bf16_pieces(a):
    hi = a.astype(BF16)
    rest = a - hi.astype(F32)
    mid = rest.astype(BF16)
    lo = (rest - mid.astype(F32)).astype(BF16)
    return hi, mid, lo


def _sel_dot(a, sel):
    n = a.shape[0]
    out = jnp.dot(jnp.concatenate(_bf16_pieces(a), axis=0), sel.astype(BF16), preferred_element_type=F32)
    return out[0:n] + out[n:2 * n] + out[2 * n:]


def _dot_sel(sel, a):
    n = a.shape[1]
    out = jnp.dot(sel.astype(BF16), jnp.concatenate(_bf16_pieces(a), axis=1), preferred_element_type=F32)
    return out[:, 0:n] + out[:, n:2 * n] + out[:, 2 * n:]


def _iota(shape, dim):
    return lax.broadcasted_iota(jnp.int32, shape, dim)


def _div(x, n):
    return x >> (n.bit_length() - 1)


def _mod(x, n):
    return x & (n - 1)


def _sigmoid(x):
    return 1.0 / (1.0 + jnp.exp(-x))


def _log_sigmoid(x):
    return jnp.minimum(x, 0.0) - jnp.log(1.0 + jnp.exp(-jnp.abs(x)))


def _seg_matrix(width, seg, value):
    r = _div(_iota((width, width), 0), seg)
    c = _div(_iota((width, width), 1), seg)
    return jnp.where(r == c, value, 0.0).astype(F32)


def _modulate(x, g, shift, scale):
    y = x * lax.rsqrt(jnp.mean(x * x, axis=-1, keepdims=True) + NORM_EPS)
    return (y * g) * (1.0 + scale) + shift


def _shift_rows(x, hp_row, hn_row):
    n = x.shape[0]
    rows = _iota(x.shape, 0)
    prev = jnp.where(rows == 0, hp_row, pltpu.roll(x, 1, 0))
    nxt = jnp.where(rows == n - 1, hn_row, pltpu.roll(x, n - 1, 0))
    return prev, nxt


def _tile_type(i, nct):
    return jnp.where(i >= nct, 1, 0)


def _seq_edges(i, nct, nt):
    first = jnp.logical_or(i == 0, i == nct)
    last = jnp.logical_or(i == nct - 1, i == nt - 1)
    return first, last


def _mod_kernel(c_ref, w_ref, b_ref, o_ref):
    cc = c_ref[...]
    s = cc * _sigmoid(cc)
    o_ref[0] = _bdot(s, w_ref[0]) + b_ref[0]


def _modulation(cc, mod_w, mod_b):
    depth, d, n = mod_w.shape
    tn = 1536
    rows = cc.shape[0]
    return pl.pallas_call(
        _mod_kernel,
        grid=(depth, n // tn),
        in_specs=[pl.BlockSpec((rows, d), lambda l, j: (0, 0)),
                  pl.BlockSpec((1, d, tn), lambda l, j: (l, 0, j)),
                  pl.BlockSpec((1, 1, tn), lambda l, j: (l, 0, j))],
        out_specs=pl.BlockSpec((1, rows, tn), lambda l, j: (l, 0, j)),
        out_shape=jax.ShapeDtypeStruct((depth, rows, n), F32),
        compiler_params=_cparams("parallel", "parallel"),
    )(cc, mod_w, mod_b.reshape(depth, 1, n))


def _inproj_kernel(x_ref, mod_ref, g_ref, wa, wb, wc, wd, pa, pb, pc, pd):
    m = mod_ref[0, 0]
    h = _modulate(x_ref[0], g_ref[...], m[0:1], m[1:2]).astype(BF16)
    pa[0] = jnp.dot(h, wa[...], preferred_element_type=F32)
    pb[0] = jnp.dot(h, wb[...], preferred_element_type=F32)
    pc[0] = jnp.dot(h, wc[...], preferred_element_type=F32)
    pd[0] = jnp.dot(h, wd[...], preferred_element_type=F32)


def _in_projection(xa, modtab, g, wa, wb, wc, wd, nct):
    b, nt_rows, d = xa.shape
    nt = nt_rows // TILE
    widths = (RW_IN, DA_IN, GLA_IN_PAD, GQA_IN)
    full = lambda w: pl.BlockSpec(w.shape, lambda bi, i: (0, 0))
    return pl.pallas_call(
        _inproj_kernel,
        grid=(b, nt),
        in_specs=[pl.BlockSpec((1, TILE, d), lambda bi, i: (bi, i, 0)),
                  pl.BlockSpec((1, 1, 6, d), lambda bi, i: (bi, _tile_type(i, nct), 0, 0)),
                  pl.BlockSpec((1, d), lambda bi, i: (0, 0)),
                  full(wa), full(wb), full(wc), full(wd)],
        out_specs=[pl.BlockSpec((1, TILE, w), lambda bi, i: (bi, i, 0)) for w in widths],
        out_shape=[jax.ShapeDtypeStruct((b, nt_rows, w), F32) for w in widths],
        compiler_params=_cparams("parallel", "parallel"),
    )(xa, modtab, g, wa, wb, wc, wd)


def _big(x, width_seg):
    t = jnp.concatenate([x, x, x, x], axis=0)
    rh = _div(_iota(t.shape, 0), CHUNK)
    ch = _div(_iota(t.shape, 1), width_seg)
    return jnp.where(rh == ch, t, 0.0)


def _collapse(x):
    return x[0:CHUNK] + x[CHUNK:2 * CHUNK] + x[2 * CHUNK:3 * CHUNK] + x[3 * CHUNK:4 * CHUNK]


def _chunk_masks(rev):
    n = 4 * CHUNK
    ri = _iota((CHUNK, CHUNK), 0)
    ci = _iota((CHUNK, CHUNK), 1)
    tri = jnp.where((ci >= ri) if rev else (ci <= ri), 1.0, 0.0).astype(F32)
    rt_row = _iota((CHUNK, n), 0)
    ct_row = _mod(_iota((CHUNK, n), 1), CHUNK)
    strict_row = (ct_row > rt_row) if rev else (ct_row < rt_row)
    incl_row = (ct_row >= rt_row) if rev else (ct_row <= rt_row)
    eye_row = ct_row == rt_row
    return tri, strict_row, incl_row, eye_row


def _lockstep(gens):
    results = [None] * len(gens)
    active = list(range(len(gens)))
    while active:
        for idx in list(active):
            try:
                next(gens[idx])
            except StopIteration as stop:
                results[idx] = stop.value
                active.remove(idx)
    return results


def _rw_chunk(rev, r, v, kk, lw, bv, kd):
    n = 4 * CHUNK
    tri, strict_row, incl_row, eye_row = _chunk_masks(rev)
    same_head = _div(_iota((n, n), 0), CHUNK) == _div(_iota((n, n), 1), CHUNK)
    c = _dot_sel(tri, lw)
    yield
    ctot = jnp.sum(lw, axis=0, keepdims=True)
    at = -kk * jnp.exp(c - lw)
    rt = r * jnp.exp(c)
    eneg = jnp.exp(-c)
    bt = bv * eneg
    kt = kd * eneg
    eend = jnp.exp(ctot - c)
    bh = bv * eend
    kh = kd * eend
    dend = jnp.exp(ctot)
    at_big = _big(at, RW_HD).astype(BF16)
    v_big = _big(v, RW_HD).astype(BF16)
    lhs = jnp.concatenate([at, rt], axis=0)
    gb = _bdot_nt(lhs, _big(bt, RW_HD))
    gk = _bdot_nt(lhs, _big(kt, RW_HD))
    yield
    nrow = jnp.where(strict_row, gb[0:CHUNK], 0.0)
    a_rb = jnp.where(incl_row, gb[CHUNK:], 0.0)
    a_ak = jnp.where(strict_row, gk[0:CHUNK], 0.0)
    a_rk = jnp.where(incl_row, gk[CHUNK:], 0.0)
    t_row = jnp.where(eye_row, 1.0, 0.0) + nrow
    q_row = _bdot(nrow, _big(nrow, CHUNK))
    wk = _bdot(jnp.concatenate([a_ak, a_rk], axis=0), v_big)
    yield
    for level in range(5):
        q_big = _big(q_row, CHUNK).astype(BF16)
        if level < 4:
            res = _bdot(jnp.concatenate([t_row, q_row], axis=0), q_big)
            t_row = t_row + res[0:CHUNK]
            q_row = res[CHUNK:]
        else:
            t_row = t_row + _bdot(t_row, q_big)
        yield
    ah = _bdot(t_row, at_big)
    uv = _bdot(t_row, _big(wk[0:CHUNK], RW_HD))
    yield
    rh = rt + _bdot(a_rb, _big(ah, RW_HD))
    yh = _bdot(a_rb, _big(uv, RW_HD)) + wk[CHUNK:]
    m_row = jnp.where(eye_row, dend, 0.0) + _collapse(jnp.where(same_head, _bdot_tn(bh, ah), 0.0))
    n_row = _collapse(jnp.where(same_head, _bdot_tn(bh, uv) + _bdot_tn(kh, v), 0.0))
    return (jnp.concatenate([m_row, rh], axis=0).astype(BF16), jnp.concatenate([n_row, yh], axis=0))


def _rw_chunk_kernel(nct, nt, pa_ref, hp_ref, hn_ref, mu_ref, w0_ref, w2_ref, a0_ref, a2_ref,
                     g2_ref, kk_ref, ka_ref, rk_ref, mrf, nyf, mrb, nyb, bonus_out, gate_out):
    i = pl.program_id(1)
    first, last = _seq_edges(i, nct, nt)
    x = pa_ref[0]
    hp = jnp.where(first, 0.0, hp_ref[0, HALO - 1:HALO, :])
    hn = jnp.where(last, 0.0, hn_ref[0, 0:1, :])
    prev, nxt = _shift_rows(x, hp, hn)
    mu = mu_ref[...]
    xm = x + mu[0:1] * (prev - x) + mu[1:2] * (nxt - x)
    r = xm[:, 0:256]
    k = xm[:, 256:512]
    v = xm[:, 512:768]
    wfb = xm[:, 768:896]
    afb = xm[:, 896:1024]
    g = xm[:, 1024:1152]
    seg = _seg_matrix(GROUP_W, RW_HD, 1.0)
    gate_out[0] = _bdot(_sigmoid(g), g2_ref[...])
    kk = k * kk_ref[...]
    kk = kk * lax.rsqrt(_sel_dot(kk * kk, seg) + 1e-12)
    wl = w0_ref[...] + _bdot(jnp.tanh(wfb), w2_ref[...])
    lw = -jnp.exp(_log_sigmoid(wl) - 0.5)
    a = _sigmoid(a0_ref[...] + _bdot(afb, a2_ref[...]))
    ka = ka_ref[...]
    kd = [k * (1.0 + (a[:, 256 * d:256 * d + 256] - 1.0) * ka) for d in range(2)]
    bv = [kk * a[:, 256 * d:256 * d + 256] for d in range(2)]
    rk = rk_ref[...]
    bonus_out[0] = _sel_dot(r * (kd[0] + kd[1]) * rk, seg) * v
    units = [(d, ck) for ck in range(TILE // CHUNK) for d in range(2)]
    gens = []
    for d, ck in units:
        rows = slice(ck * CHUNK, (ck + 1) * CHUNK)
        gens.append(_rw_chunk(d == 1, r[rows], v[rows], kk[rows], lw[rows, 256 * d:256 * d + 256],
                              bv[d][rows], kd[d][rows]))
    for (d, ck), (mr, ny) in zip(units, _lockstep(gens)):
        mr_out, ny_out = ((mrf, nyf), (mrb, nyb))[d]
        mr_out[0, 2 * ck * CHUNK:2 * (ck + 1) * CHUNK, :] = mr
        ny_out[0, 2 * ck * CHUNK:2 * (ck + 1) * CHUNK, :] = ny


def _rwkv_chunks(pa, rw, nct):
    b, nt_rows, _ = pa.shape
    nt = nt_rows // TILE
    hb = TILE // HALO
    nhb = nt_rows // HALO
    full = lambda w: pl.BlockSpec(w.shape, lambda bi, i: (0,) * w.ndim)
    tile = lambda rows, w: pl.BlockSpec((1, rows, w), lambda bi, i: (bi, i, 0))
    consts = (rw["mu"], rw["w0"], rw["w2"], rw["a0"], rw["a2"], rw["g2"], rw["kk"], rw["ka"], rw["rk"])
    pair = lambda dt: jax.ShapeDtypeStruct((b, 2 * nt_rows, 256), dt)
    return pl.pallas_call(
        functools.partial(_rw_chunk_kernel, nct, nt),
        grid=(b, nt),
        in_specs=[tile(TILE, RW_IN),
                  pl.BlockSpec((1, HALO, RW_IN), lambda bi, i: (bi, jnp.maximum(i * hb - 1, 0), 0)),
                  pl.BlockSpec((1, HALO, RW_IN), lambda bi, i: (bi, jnp.minimum(i * hb + hb, nhb - 1), 0))]
                 + [full(w) for w in consts],
        out_specs=[tile(2 * TILE, 256)] * 4 + [tile(TILE, 256)] * 2,
        out_shape=[pair(BF16), pair(F32), pair(BF16), pair(F32),
                   jax.ShapeDtypeStruct((b, nt_rows, 256), F32), jax.ShapeDtypeStruct((b, nt_rows, 256), F32)],
        compiler_params=_cparams("parallel", "parallel"),
    )(pa, pa, pa, *consts)


def _gla_chunk(rev, pc, a2, ab):
    n = 4 * CHUNK
    tri, _, incl_row, _ = _chunk_masks(rev)
    q = pc[:, 0:128] * (GLA_DK ** -0.5)
    k = pc[:, 128:256]
    v = pc[:, 256:512]
    lg = _log_sigmoid(_bdot(pc[:, 768:896], a2) + ab) / GLA_TAU
    yield
    bcum = _dot_sel(tri, lg)
    yield
    btot = jnp.sum(lg, axis=0, keepdims=True)
    q_in = q * jnp.exp(bcum)
    k_in = k * jnp.exp(-bcum)
    k_end = k * jnp.exp(btot - bcum)
    dec = jnp.exp(btot)
    att = jnp.where(incl_row, _bdot_nt(q_in, _big(k_in, GLA_DK)), 0.0)
    same_head = _div(_iota((n, GLA_KW), 0), 64) == _div(_iota((n, GLA_KW), 1), GLA_DK)
    inc = _collapse(jnp.where(same_head, _bdot_tn(v, k_end), 0.0))
    yield
    o_intra = _bdot(att, _big(v, 64))
    return q_in.astype(BF16), o_intra, inc, jnp.broadcast_to(dec, (8, GLA_KW))


def _gla_chunk_kernel(pc_ref, a2_ref, ab_ref, qf, of, nf, df, qb, ob, nb, db):
    pc = pc_ref[0]
    units = [(d, ck) for ck in range(TILE // CHUNK) for d in range(2)]
    gens = [_gla_chunk(d == 1, pc[ck * CHUNK:(ck + 1) * CHUNK], a2_ref[d], ab_ref[d]) for d, ck in units]
    for (d, ck), res in zip(units, _lockstep(gens)):
        outs = ((qf, of, nf, df), (qb, ob, nb, db))[d]
        for ref, val in zip(outs[:3], res[:3]):
            ref[0, ck * CHUNK:(ck + 1) * CHUNK, :] = val
        outs[3][0, 8 * ck:8 * ck + 8, :] = res[3]


def _gla_chunks(pc, a2pad, ab):
    b, nt_rows, w = pc.shape
    nt = nt_rows // TILE
    nc8 = nt_rows // CHUNK * 8
    tile = lambda rows, w_: pl.BlockSpec((1, rows, w_), lambda bi, i: (bi, i, 0))
    one = [tile(TILE, GLA_KW), tile(TILE, 256), tile(TILE, GLA_KW), tile(TILE // CHUNK * 8, GLA_KW)]
    shp = [jax.ShapeDtypeStruct((b, nt_rows, GLA_KW), BF16), jax.ShapeDtypeStruct((b, nt_rows, 256), F32),
           jax.ShapeDtypeStruct((b, nt_rows, GLA_KW), F32), jax.ShapeDtypeStruct((b, nc8, GLA_KW), F32)]
    return pl.pallas_call(
        _gla_chunk_kernel,
        grid=(b, nt),
        in_specs=[tile(TILE, w),
                  pl.BlockSpec(a2pad.shape, lambda bi, i: (0, 0, 0)),
                  pl.BlockSpec(ab.shape, lambda bi, i: (0, 0, 0))],
        out_specs=one + one,
        out_shape=shp + shp,
        compiler_params=_cparams("parallel", "parallel"),
    )(pc, a2pad, ab)


def _state_kernel(bg, mrf, nyf, mrb, nyb, qf, oif, gnf, dcf, qb, oib, gnb, dcb, yf, yb, of, ob, srw, sgl):
    @pl.when(pl.program_id(1) == 0)
    def _():
        srw[...] = jnp.zeros_like(srw)
        sgl[...] = jnp.zeros_like(sgl)

    for d, (mr, ny, y_out) in enumerate(((mrf, nyf, yf), (mrb, nyb, yb))):
        for j in range(bg):
            s_big = _big(srw[d, j], RW_HD).astype(BF16)
            res = jnp.dot(mr[j], s_big, preferred_element_type=F32) + ny[j]
            srw[d, j] = res[0:CHUNK]
            y_out[j] = res[CHUNK:]
    for d, (q, oi, gn, dc, o_out) in enumerate(((qf, oif, gnf, dcf, of), (qb, oib, gnb, dcb, ob))):
        for j in range(bg):
            st = sgl[d, j]
            o_out[j] = oi[j] + _bdot_nt(q[j], _big(st, GLA_DK))
            sgl[d, j] = st * dc[j, 0:1, :] + gn[j]


def _scan_order(ncc, nc):
    fwd = lambda s: s
    bwd = lambda s: jnp.where(s < ncc, ncc - 1 - s, nc - 1 - (s - ncc))
    return fwd, bwd


def _state_pass(rw_terms, gla_terms, ncc):
    mrf, nyf, mrb, nyb = rw_terms
    b = mrf.shape[0]
    bg = math.gcd(b, STATE_BATCH)
    nt_rows = mrf.shape[1] // 2
    nc = nt_rows // CHUNK
    fwd, bwd = _scan_order(ncc, nc)
    blk = lambda rows, w, order: pl.BlockSpec((bg, rows, w), lambda bi, s: (bi, order(s), 0))
    rw_in = lambda order: [blk(2 * CHUNK, 256, order)] * 2
    gla_in = lambda order: [blk(CHUNK, GLA_KW, order), blk(CHUNK, 256, order), blk(CHUNK, GLA_KW, order),
                            blk(8, GLA_KW, order)]
    out = lambda order: blk(CHUNK, 256, order)
    return pl.pallas_call(
        functools.partial(_state_kernel, bg),
        grid=(b // bg, nc),
        in_specs=rw_in(fwd) + rw_in(bwd) + gla_in(fwd) + gla_in(bwd),
        out_specs=[out(fwd), out(bwd), out(fwd), out(bwd)],
        out_shape=[jax.ShapeDtypeStruct((b, nt_rows, 256), F32)] * 4,
        scratch_shapes=[pltpu.VMEM((2, bg, CHUNK, 256), F32), pltpu.VMEM((2, bg, CHUNK, GLA_KW), F32)],
        compiler_params=_cparams("parallel", "arbitrary"),
    )(mrf, nyf, mrb, nyb, *gla_terms)


def _norm_rope(x, seg, gain, cos, sin, half):
    w = x.shape[1]
    xn = x * lax.rsqrt(_sel_dot(x * x, _seg_matrix(w, seg, 1.0 / seg)) + NORM_EPS) * gain
    lane = _mod(_iota(x.shape, 1), 2 * half)
    partner = jnp.where(lane < half, pltpu.roll(xn, w - half, 1), pltpu.roll(xn, half, 1))
    return xn * cos + partner * sin


def _qk_prep_kernel(pb_ref, pd_ref, cda_ref, sda_ref, cgq_ref, sgq_ref, gda_ref, ggq_ref,
                    daqt, dak, davt, gqqt, gqk, gqvt):
    pb = pb_ref[0]
    pd = pd_ref[0]
    cda = cda_ref[...]
    sda = sda_ref[...]
    cgq = cgq_ref[...]
    sgq = sgq_ref[...]
    gda = gda_ref[...]
    ggq = ggq_ref[...]
    q = _norm_rope(pb[:, 0:256], DA_HD, gda[0:1], cda, sda, DA_HD // 2)
    daqt[0] = (q * (DA_HD ** -0.5 * LOG2E)).T.astype(BF16)
    dak[0] = _norm_rope(pb[:, 256:512], DA_HD, gda[1:2], cda, sda, DA_HD // 2).astype(BF16)
    davt[0] = pb[:, 512:768].T.astype(BF16)
    q = _norm_rope(pd[:, 0:256], GQA_HD, ggq[0:1], cgq, sgq, GQA_HD // 2)
    gqqt[0] = (q * (GQA_HD ** -0.5 * LOG2E)).T.astype(BF16)
    gqk[0] = _norm_rope(pd[:, 256:384], GQA_HD, ggq[1:2, 0:128], cgq[:, 0:128], sgq[:, 0:128],
                        GQA_HD // 2).astype(BF16)
    gqvt[0] = pd[:, 384:512].T.astype(BF16)


def _qk_prepare(pb, pd, cda, sda, cgq, sgq, gda, ggq):
    b, nt_rows, _ = pb.shape
    nt = nt_rows // TILE
    tile = lambda w: pl.BlockSpec((1, TILE, w), lambda bi, i: (bi, i, 0))
    tab = pl.BlockSpec((TILE, 256), lambda bi, i: (i, 0))
    gain = pl.BlockSpec((2, 256), lambda bi, i: (0, 0))
    tr = lambda w: pl.BlockSpec((1, w, TILE), lambda bi, i: (bi, 0, i))
    rowmajor = lambda w: jax.ShapeDtypeStruct((b, nt_rows, w), BF16)
    transposed = lambda w: jax.ShapeDtypeStruct((b, w, nt_rows), BF16)
    return pl.pallas_call(
        _qk_prep_kernel,
        grid=(b, nt),
        in_specs=[tile(DA_IN), tile(GQA_IN), tab, tab, tab, tab, gain, gain],
        out_specs=[tr(256), tile(256), tr(256), tr(256), tile(GQA_KVW), tr(GQA_KVW)],
        out_shape=[transposed(256), rowmajor(256), transposed(256),
                   transposed(256), rowmajor(GQA_KVW), transposed(GQA_KVW)],
        compiler_params=_cparams("parallel", "parallel"),
    )(pb, pd, cda, sda, cgq, sgq, gda, ggq)


def _staggered(gens):
    results = [None] * len(gens)
    next(gens[0])
    for u, gen in enumerate(gens):
        if u + 1 < len(gens):
            next(gens[u + 1])
        next(gen)
        try:
            next(gen)
        except StopIteration as stop:
            results[u] = stop.value
    return results


def _softmax_pv_t(load_k, load_wq, load_vt):
    s = jnp.dot(load_k(), load_wq(), preferred_element_type=F32)
    yield
    e = jnp.exp2(s - jnp.max(s, axis=0, keepdims=True))
    l = jnp.sum(e, axis=0, keepdims=True)
    e = e.astype(BF16)
    yield
    return jnp.dot(load_vt(), e, preferred_element_type=F32) / l


def _feature_rows(qt_ref, lo, width):
    qt = qt_ref[0]
    rows = _iota(qt.shape, 0)
    keep = jnp.logical_and(rows >= lo, rows < lo + width)
    return jnp.where(keep, qt, jnp.zeros_like(qt))


def _da_heads(lam_init, qt_ref, k_ref, vt_ref, lam_ref, g_ref, nk):
    lv = lam_ref[...]
    lam = (jnp.exp(jnp.sum(lv[0:1] * lv[1:2], axis=1, keepdims=True))
           - jnp.exp(jnp.sum(lv[2:3] * lv[3:4], axis=1, keepdims=True)) + lam_init)
    gens = []
    for h in range(4):
        for m in range(2):
            lo = 64 * h + 32 * m
            gens.append(_softmax_pv_t(lambda: k_ref[0, 0:nk, :],
                                      lambda lo=lo: _feature_rows(qt_ref, lo, DA_HD),
                                      lambda h=h: vt_ref[0, 64 * h:64 * h + 64, 0:nk]))
    o = _staggered(gens)
    outs = []
    for h in range(4):
        oh = (o[2 * h] - lam * o[2 * h + 1]).T
        oh = oh * lax.rsqrt(jnp.mean(oh * oh, axis=-1, keepdims=True) + NORM_EPS) * g_ref[...]
        outs.append(oh * (1.0 - lam_init))
    return jnp.concatenate(outs, axis=1)


def _gqa_heads(qt_ref, k_ref, vt_ref, nk):
    def wq(h):
        qh = qt_ref[0, 64 * h:64 * h + 64, :]
        zero = jnp.zeros_like(qh)
        return jnp.concatenate([qh, zero] if h // 2 == 0 else [zero, qh], axis=0)

    gens = [_softmax_pv_t(lambda: k_ref[0, 0:nk, :], lambda h=h: wq(h),
                          lambda g=h // 2: vt_ref[0, 64 * g:64 * g + 64, 0:nk]) for h in range(4)]
    return jnp.concatenate([o.T for o in _staggered(gens)], axis=1)


def _attn_kernel(heads_fn, q_tile0, nct, ctx_len, nk_all, qt_ref, k_ref, vt_ref, *rest):
    *extra, o_ref = rest
    i = pl.program_id(1) + q_tile0

    @pl.when(i < nct)
    def _():
        o_ref[0] = heads_fn(qt_ref, k_ref, vt_ref, *extra, ctx_len)

    @pl.when(i >= nct)
    def _():
        o_ref[0] = heads_fn(qt_ref, k_ref, vt_ref, *extra, nk_all)


def _attention(heads_fn, qt, k, vt, extra, q_tile0, nct):
    b, _, nt_rows = qt.shape
    nqt = nt_rows // TILE - q_tile0
    full = lambda w: pl.BlockSpec(w.shape, lambda bi, i: (0,) * w.ndim)
    return pl.pallas_call(
        functools.partial(_attn_kernel, heads_fn, q_tile0, nct, nct * TILE, nt_rows),
        grid=(b, nqt),
        in_specs=[pl.BlockSpec((1, 256, TILE), lambda bi, i: (bi, 0, i + q_tile0)),
                  pl.BlockSpec((1,) + k.shape[1:], lambda bi, i: (bi, 0, 0)),
                  pl.BlockSpec((1,) + vt.shape[1:], lambda bi, i: (bi, 0, 0))]
                 + [full(w) for w in extra],
        out_specs=pl.BlockSpec((1, TILE, 256), lambda bi, i: (bi, i, 0)),
        out_shape=jax.ShapeDtypeStruct((b, nqt * TILE, 256), F32),
        compiler_params=_cparams("parallel", "parallel"),
    )(qt, k, vt, *extra)


def _outproj_kernel(x_ref, mod_ref, yf, yb, bonus, gate, lng, lnb, ob, of, obk, rr, glag, od, w_ref, o_ref):
    m = mod_ref[0, 0]
    seg = _seg_matrix(GROUP_W, 64, 1.0 / 64)
    y = yf[0] + yb[0]
    mean = _sel_dot(y, seg)
    yc = y - mean
    var = _sel_dot(yc * yc, seg)
    oa = ((yc * lax.rsqrt(var + RW_LN_EPS)) * lng[...] + lnb[...] + bonus[0]) * gate[0]
    o = of[0] + obk[0]
    r = rr[0]
    oc = (o * lax.rsqrt(_sel_dot(o * o, seg) + NORM_EPS) * glag[...]) * (r * _sigmoid(r))
    w = w_ref[...]
    mix = (_bdot(oa, w[0:256]) + _bdot(ob[0], w[256:512]) + _bdot(oc, w[512:768]) + _bdot(od[0], w[768:1024]))
    o_ref[0] = x_ref[0] + m[2:3] * mix


def _out_projection(xa, modtab, yf, yb, bonus, gate, lng, lnb, ob, of, obk, pc, glag, od, w, t0, nct):
    b, nt_rows, d = xa.shape
    n = nt_rows // TILE - t0
    tile = lambda w_, off: pl.BlockSpec((1, TILE, w_), lambda bi, i: (bi, i + off, 0))
    row = pl.BlockSpec((1, 256), lambda bi, i: (0, 0))
    return pl.pallas_call(
        _outproj_kernel,
        grid=(b, n),
        in_specs=[tile(d, t0),
                  pl.BlockSpec((1, 1, 6, d), lambda bi, i: (bi, _tile_type(i + t0, nct), 0, 0)),
                  tile(256, t0), tile(256, t0), tile(256, t0), tile(256, t0), row, row,
                  tile(256, 0), tile(256, t0), tile(256, t0),
                  pl.BlockSpec((1, TILE, 256), lambda bi, i: (bi, i + t0, 2)), row,
                  tile(256, 0),
                  pl.BlockSpec(w.shape, lambda bi, i: (0, 0))],
        out_specs=pl.BlockSpec((1, TILE, d), lambda bi, i: (bi, i, 0)),
        out_shape=jax.ShapeDtypeStruct((b, n * TILE, d), F32),
        compiler_params=_cparams("parallel", "parallel"),
    )(xa, modtab, yf, yb, bonus, gate, lng, lnb, ob, of, obk, pc, glag, od, w)


def _ffn_kernel(t0, nct, nt, x_ref, hp_ref, hn_ref, mod_ref, g_ref, wu_ref, wg_ref, cw_ref, cb_ref, wd_ref, o_ref):
    i = pl.program_id(1) + t0
    first, last = _seq_edges(i, nct, nt)
    m = mod_ref[0, 0]
    g = g_ref[...]
    x = x_ref[0]
    h = _modulate(x, g, m[3:4], m[4:5])
    hp = jnp.where(first, 0.0, _modulate(hp_ref[0], g, m[3:4], m[4:5]))
    hn = jnp.where(last, 0.0, _modulate(hn_ref[0], g, m[3:4], m[4:5]))
    hext = jnp.concatenate([hp, h, hn], axis=0).astype(BF16)
    gext = jnp.dot(hext, wg_ref[...], preferred_element_type=F32)
    n = TILE + 2 * HALO
    prev = pltpu.roll(gext, 1, 0)[HALO:HALO + TILE]
    gg = gext[HALO:HALO + TILE]
    nxt = pltpu.roll(gext, n - 1, 0)[HALO:HALO + TILE]
    u = jnp.dot(h.astype(BF16), wu_ref[...], preferred_element_type=F32)
    cw = cw_ref[...]
    gc = cw[0:1] * prev + cw[1:2] * gg + cw[2:3] * nxt + cb_ref[...]
    act = (gc * _sigmoid(gc)) * u
    o_ref[0] = x + m[5:6] * _bdot(act, wd_ref[...])


def _ffn(xa, modtab, g, wu, wg, cw, cb, wd, t0, nct, nt):
    b, rows, d = xa.shape
    n = rows // TILE
    hb = TILE // HALO
    nhb = rows // HALO
    const = lambda w: pl.BlockSpec(w.shape, lambda bi, i: (0, 0), pipeline_mode=pl.Buffered(1))
    return pl.pallas_call(
        functools.partial(_ffn_kernel, t0, nct, nt),
        grid=(b, n),
        in_specs=[pl.BlockSpec((1, TILE, d), lambda bi, i: (bi, i, 0)),
                  pl.BlockSpec((1, HALO, d), lambda bi, i: (bi, jnp.maximum(i * hb - 1, 0), 0)),
                  pl.BlockSpec((1, HALO, d), lambda bi, i: (bi, jnp.minimum(i * hb + hb, nhb - 1), 0)),
                  pl.BlockSpec((1, 1, 6, d), lambda bi, i: (bi, _tile_type(i + t0, nct), 0, 0)),
                  pl.BlockSpec((1, d), lambda bi, i: (0, 0)),
                  const(wu), const(wg),
                  pl.BlockSpec(cw.shape, lambda bi, i: (0, 0)),
                  pl.BlockSpec(cb.shape, lambda bi, i: (0, 0)),
                  const(wd)],
        out_specs=pl.BlockSpec((1, TILE, d), lambda bi, i: (bi, i, 0)),
        out_shape=jax.ShapeDtypeStruct((b, rows, d), F32),
        compiler_params=_cparams("parallel", "parallel"),
    )(xa, xa, xa, modtab, g, wu, wg, cw, cb, wd)


def _rope_tables(ctx_len, rows, head_dim):
    row = jnp.repeat(jnp.arange(rows, dtype=F32), GRID_W)
    col = jnp.tile(jnp.arange(GRID_W, dtype=F32), rows)
    n_freq = head_dim // 4
    inv = ROPE_THETA ** (-jnp.arange(n_freq, dtype=F32) / n_freq)
    ang = jnp.concatenate([row[:, None] * inv, col[:, None] * inv], axis=-1)
    cos = jnp.concatenate([jnp.cos(ang), jnp.cos(ang)], axis=-1)
    sin = jnp.concatenate([-jnp.sin(ang), jnp.sin(ang)], axis=-1)
    reps = 256 // head_dim
    cos = jnp.tile(cos, (1, reps))
    sin = jnp.tile(sin, (1, reps))
    cos = jnp.concatenate([jnp.ones((ctx_len, 256), F32), cos], axis=0)
    sin = jnp.concatenate([jnp.zeros((ctx_len, 256), F32), sin], axis=0)
    return cos, sin


def _block_diag2(m0, m1):
    z01 = jnp.zeros((m0.shape[0], m1.shape[1]), m0.dtype)
    z10 = jnp.zeros((m1.shape[0], m0.shape[1]), m0.dtype)
    return jnp.concatenate([jnp.concatenate([m0, z01], axis=1), jnp.concatenate([z10, m1], axis=1)], axis=0)


def _gla_in_weights(w):
    d = w.shape[0]
    q, k, v, gf, gb, r = jnp.split(w, [128, 256, 512, 528, 544], axis=1)
    return jnp.concatenate([q, k, v, r, gf, gb, jnp.zeros((d, GLA_IN_PAD - GLA_IN), w.dtype)], axis=1)


def kernel(x, c, ctx, c_ctx, mod_w, mod_b, norm_mix_g, norm_ffn_g, w_in, w_out, rw_mu, rw_w0, rw_w2, rw_a0,
           rw_a2, rw_g2, rw_kk, rw_ka, rw_rk, rw_ln_g, rw_ln_b, da_qk_g, da_lam, da_subln_g, gla_a2, gla_ab,
           gla_norm_g, gqa_qk_g, ffn_w_up, ffn_conv_w, ffn_conv_b, ffn_w_down):
    b, seq, d = x.shape
    ctx_len = ctx.shape[1]
    depth = mod_w.shape[0]
    assert d == D_MODEL and seq % TILE == 0 and ctx_len % TILE == 0 and seq % GRID_W == 0
    nt_rows = ctx_len + seq
    nt = nt_rows // TILE
    nct = ctx_len // TILE
    ncc = ctx_len // CHUNK

    rows = -(-(b + 1) // 8) * 8
    cc = jnp.concatenate([c, c_ctx[None, :], jnp.zeros((rows - b - 1, d), F32)], axis=0)
    mods = _modulation(cc, mod_w, mod_b).reshape(depth, rows, 6, d)

    cda, sda = _rope_tables(ctx_len, seq // GRID_W, DA_HD)
    cgq, sgq = _rope_tables(ctx_len, seq // GRID_W, GQA_HD)

    xa = jnp.concatenate([ctx, x], axis=1)
    for i in range(depth):
        last = i == depth - 1
        t0 = nct if last else 0
        modtab = jnp.stack([jnp.broadcast_to(mods[i, b], (b, 6, d)), mods[i, :b]], axis=1)
        row = lambda v: v.reshape(1, -1)

        wi = w_in[i]
        wa = wi[:, 0:RW_IN].astype(BF16)
        wb = wi[:, RW_IN:RW_IN + DA_IN].astype(BF16)
        wc = _gla_in_weights(wi[:, RW_IN + DA_IN:RW_IN + DA_IN + GLA_IN]).astype(BF16)
        wd = wi[:, RW_IN + DA_IN + GLA_IN:].astype(BF16)
        pa, pb, pc, pd = _in_projection(xa, modtab, row(norm_mix_g[i]), wa, wb, wc, wd, nct)

        rw = dict(mu=rw_mu[i], w0=row(rw_w0[i]), w2=_block_diag2(rw_w2[i, 0], rw_w2[i, 1]),
                  a0=row(rw_a0[i]), a2=_block_diag2(rw_a2[i, 0], rw_a2[i, 1]), g2=rw_g2[i],
                  kk=row(rw_kk[i]), ka=row(rw_ka[i]), rk=row(rw_rk[i]))
        mrf, nyf, mrb, nyb, bonus, gate = _rwkv_chunks(pa, rw, nct)

        a2pad = jnp.zeros((2, 128, GLA_KW), F32)
        a2pad = a2pad.at[0, 0:16].set(gla_a2[i, 0]).at[1, 16:32].set(gla_a2[i, 1])
        gla_terms = _gla_chunks(pc, a2pad, gla_ab[i].reshape(2, 1, GLA_KW))
        yf, yb, of, obk = _state_pass((mrf, nyf, mrb, nyb), gla_terms, ncc)

        gda = jnp.tile(da_qk_g[i], (1, 256 // DA_HD))
        ggq = jnp.tile(gqa_qk_g[i], (1, 256 // GQA_HD))
        daqt, dak, davt, gqqt, gqk, gqvt = _qk_prepare(pb, pd, cda, sda, cgq, sgq, gda, ggq)
        lam_init = 0.8 - 0.6 * math.exp(-0.3 * i)
        ob = _attention(functools.partial(_da_heads, lam_init), daqt, dak, davt,
                        (da_lam[i], row(da_subln_g[i])), t0, nct)
        od = _attention(_gqa_heads, gqqt, gqk, gqvt, (), t0, nct)

        xa = _out_projection(xa, modtab, yf, yb, bonus, gate, row(rw_ln_g[i]), row(rw_ln_b[i]), ob, of, obk, pc,
                             row(jnp.tile(gla_norm_g[i], 4)), od, w_out[i].astype(BF16), t0, nct)
        wu = ffn_w_up[i]
        xa = _ffn(xa, modtab, row(norm_ffn_g[i]), wu[:, :D_FF].astype(BF16), wu[:, D_FF:].astype(BF16),
                  ffn_conv_w[i], row(ffn_conv_b[i]), ffn_w_down[i].astype(BF16), t0, nct, nt)
    return xa
```

```python
import functools
import math

import jax
import jax.numpy as jnp
from jax import lax
from jax.experimental import pallas as pl
from jax.experimental.pallas import tpu as pltpu

F32 = jnp.float32
BF16 = jnp.bfloat16
LOG2E = math.log2(math.e)

D_MODEL = 1024
GRID_W = 64
GROUP_W = 256
NORM_EPS = 1e-6
ROPE_THETA = 10000.0
RW_HD = 64
RW_IN = 1152
RW_LN_EPS = 64e-5
DA_HD = 32
DA_IN = 768
GLA_DK = 32
GLA_KW = 128
GLA_TAU = 16.0
GLA_IN = 800
GLA_IN_PAD = 896
GQA_HD = 64
GQA_KVW = 128
GQA_IN = 512
D_FF = 2816

CHUNK = 64
TILE = 256
HALO = 8
STATE_BATCH = 8
VMEM_LIMIT = 56 * 1024 * 1024


def _cparams(*sem):
    return pltpu.CompilerParams(dimension_semantics=sem, vmem_limit_bytes=VMEM_LIMIT)


def _bdot(a, b):
    return jnp.dot(a.astype(BF16), b.astype(BF16), preferred_element_type=F32)


def _bdot_nt(a, b):
    return lax.dot_general(a.astype(BF16), b.astype(BF16), (((1,), (1,)), ((), ())),
                           preferred_element_type=F32)


def _bdot_tn(a, b):
    return jnp.dot(a.astype(F32).T.astype(BF16), b.astype(BF16), preferred_element_type=F32)


def _bf16_pieces(a):
    hi = a.astype(BF16)
    rest = a - hi.astype(F32)
    mid = rest.astype(BF16)
    lo = (rest - mid.astype(F32)).astype(BF16)
    return hi, mid, lo


def _sel_dot(a, sel):
    n = a.shape[0]
    out = jnp.dot(jnp.concatenate(_bf16_pieces(a), axis=0), sel.astype(BF16), preferred_element_type=F32)
    return out[0:n] + out[n:2 * n] + out[2 * n:]


def _dot_sel(sel, a):
    n = a.shape[1]
    out = jnp.dot(sel.astype(BF16), jnp.concatenate(_bf16_pieces(a), axis=1), preferred_element_type=F32)
    return out[:, 0:n] + out[:, n:2 * n] + out[:, 2 * n:]


def _iota(shape, dim):
    return lax.broadcasted_iota(jnp.int32, shape, dim)


def _div(x, n):
    return x >> (n.bit_length() - 1)


def _mod(x, n):
    return x & (n - 1)


def _sigmoid(x):
    return 1.0 / (1.0 + jnp.exp(-x))


def _log_sigmoid(x):
    return jnp.minimum(x, 0.0) - jnp.log(1.0 + jnp.exp(-jnp.abs(x)))


def _seg_matrix(width, seg, value):
    r = _div(_iota((width, width), 0), seg)
    c = _div(_iota((width, width), 1), seg)
    return jnp.where(r == c, value, 0.0).astype(F32)


def _modulate(x, g, shift, scale):
    y = x * lax.rsqrt(jnp.mean(x * x, axis=-1, keepdims=True) + NORM_EPS)
    return (y * g) * (1.0 + scale) + shift


def _shift_rows(x, hp_row, hn_row):
    n = x.shape[0]
    rows = _iota(x.shape, 0)
    prev = jnp.where(rows == 0, hp_row, pltpu.roll(x, 1, 0))
    nxt = jnp.where(rows == n - 1, hn_row, pltpu.roll(x, n - 1, 0))
    return prev, nxt


def _tile_type(i, nct):
    return jnp.where(i >= nct, 1, 0)


def _seq_edges(i, nct, nt):
    first = jnp.logical_or(i == 0, i == nct)
    last = jnp.logical_or(i == nct - 1, i == nt - 1)
    return first, last


def _mod_kernel(c_ref, w_ref, b_ref, o_ref):
    cc = c_ref[...]
    s = cc * _sigmoid(cc)
    o_ref[0] = _bdot(s, w_ref[0]) + b_ref[0]


def _modulation(cc, mod_w, mod_b):
    depth, d, n = mod_w.shape
    tn = 1536
    rows = cc.shape[0]
    return pl.pallas_call(
        _mod_kernel,
        grid=(depth, n // tn),
        in_specs=[pl.BlockSpec((rows, d), lambda l, j: (0, 0)),
                  pl.BlockSpec((1, d, tn), lambda l, j: (l, 0, j)),
                  pl.BlockSpec((1, 1, tn), lambda l, j: (l, 0, j))],
        out_specs=pl.BlockSpec((1, rows, tn), lambda l, j: (l, 0, j)),
        out_shape=jax.ShapeDtypeStruct((depth, rows, n), F32),
        compiler_params=_cparams("parallel", "parallel"),
    )(cc, mod_w, mod_b.reshape(depth, 1, n))


def _big(x, width_seg):
    t = jnp.concatenate([x, x, x, x], axis=0)
    rh = _div(_iota(t.shape, 0), CHUNK)
    ch = _div(_iota(t.shape, 1), width_seg)
    return jnp.where(rh == ch, t, 0.0)


def _collapse(x):
    return x[0:CHUNK] + x[CHUNK:2 * CHUNK] + x[2 * CHUNK:3 * CHUNK] + x[3 * CHUNK:4 * CHUNK]


def _chunk_masks(rev):
    n = 4 * CHUNK
    ri = _iota((CHUNK, CHUNK), 0)
    ci = _iota((CHUNK, CHUNK), 1)
    tri = jnp.where((ci >= ri) if rev else (ci <= ri), 1.0, 0.0).astype(F32)
    rt_row = _iota((CHUNK, n), 0)
    ct_row = _mod(_iota((CHUNK, n), 1), CHUNK)
    strict_row = (ct_row > rt_row) if rev else (ct_row < rt_row)
    incl_row = (ct_row >= rt_row) if rev else (ct_row <= rt_row)
    eye_row = ct_row == rt_row
    return tri, strict_row, incl_row, eye_row


def _lockstep(gens):
    results = [None] * len(gens)
    active = list(range(len(gens)))
    while active:
        for idx in list(active):
            try:
                next(gens[idx])
            except StopIteration as stop:
                results[idx] = stop.value
                active.remove(idx)
    return results


def _rw_chunk(rev, r, v, kk, lw, bv, kd):
    n = 4 * CHUNK
    tri, strict_row, incl_row, eye_row = _chunk_masks(rev)
    same_head = _div(_iota((n, n), 0), CHUNK) == _div(_iota((n, n), 1), CHUNK)
    c = _dot_sel(tri, lw)
    yield
    ctot = jnp.sum(lw, axis=0, keepdims=True)
    at = -kk * jnp.exp(c - lw)
    rt = r * jnp.exp(c)
    eneg = jnp.exp(-c)
    bt = bv * eneg
    kt = kd * eneg
    eend = jnp.exp(ctot - c)
    bh = bv * eend
    kh = kd * eend
    dend = jnp.exp(ctot)
    at_big = _big(at, RW_HD).astype(BF16)
    v_big = _big(v, RW_HD).astype(BF16)
    lhs = jnp.concatenate([at, rt], axis=0)
    gb = _bdot_nt(lhs, _big(bt, RW_HD))
    gk = _bdot_nt(lhs, _big(kt, RW_HD))
    yield
    nrow = jnp.where(strict_row, gb[0:CHUNK], 0.0)
    a_rb = jnp.where(incl_row, gb[CHUNK:], 0.0)
    a_ak = jnp.where(strict_row, gk[0:CHUNK], 0.0)
    a_rk = jnp.where(incl_row, gk[CHUNK:], 0.0)
    t_row = jnp.where(eye_row, 1.0, 0.0) + nrow
    q_row = _bdot(nrow, _big(nrow, CHUNK))
    wk = _bdot(jnp.concatenate([a_ak, a_rk], axis=0), v_big)
    yield
    for level in range(5):
        q_big = _big(q_row, CHUNK).astype(BF16)
        if level < 4:
            res = _bdot(jnp.concatenate([t_row, q_row], axis=0), q_big)
            t_row = t_row + res[0:CHUNK]
            q_row = res[CHUNK:]
        else:
            t_row = t_row + _bdot(t_row, q_big)
        yield
    ah = _bdot(t_row, at_big)
    uv = _bdot(t_row, _big(wk[0:CHUNK], RW_HD))
    yield
    rh = rt + _bdot(a_rb, _big(ah, RW_HD))
    yh = _bdot(a_rb, _big(uv, RW_HD)) + wk[CHUNK:]
    m_row = jnp.where(eye_row, dend, 0.0) + _collapse(jnp.where(same_head, _bdot_tn(bh, ah), 0.0))
    n_row = _collapse(jnp.where(same_head, _bdot_tn(bh, uv) + _bdot_tn(kh, v), 0.0))
    return (jnp.concatenate([m_row, rh], axis=0).astype(BF16), jnp.concatenate([n_row, yh], axis=0))


def _rw_chunk_kernel(nct, nt, pa_ref, hp_ref, hn_ref, mu_ref, w0_ref, w2_ref, a0_ref, a2_ref,
                     g2_ref, kk_ref, ka_ref, rk_ref, mrf, nyf, mrb, nyb, bonus_out, gate_out):
    i = pl.program_id(1)
    first, last = _seq_edges(i, nct, nt)
    x = pa_ref[0]
    hp = jnp.where(first, 0.0, hp_ref[0, HALO - 1:HALO, :])
    hn = jnp.where(last, 0.0, hn_ref[0, 0:1, :])
    prev, nxt = _shift_rows(x, hp, hn)
    mu = mu_ref[...]
    xm = x + mu[0:1] * (prev - x) + mu[1:2] * (nxt - x)
    r = xm[:, 0:256]
    k = xm[:, 256:512]
    v = xm[:, 512:768]
    wfb = xm[:, 768:896]
    afb = xm[:, 896:1024]
    g = xm[:, 1024:1152]
    seg = _seg_matrix(GROUP_W, RW_HD, 1.0)
    gate_out[0] = _bdot(_sigmoid(g), g2_ref[...])
    kk = k * kk_ref[...]
    kk = kk * lax.rsqrt(_sel_dot(kk * kk, seg) + 1e-12)
    wl = w0_ref[...] + _bdot(jnp.tanh(wfb), w2_ref[...])
    lw = -jnp.exp(_log_sigmoid(wl) - 0.5)
    a = _sigmoid(a0_ref[...] + _bdot(afb, a2_ref[...]))
    ka = ka_ref[...]
    kd = [k * (1.0 + (a[:, 256 * d:256 * d + 256] - 1.0) * ka) for d in range(2)]
    bv = [kk * a[:, 256 * d:256 * d + 256] for d in range(2)]
    rk = rk_ref[...]
    bonus_out[0] = _sel_dot(r * (kd[0] + kd[1]) * rk, seg) * v
    units = [(d, ck) for ck in range(TILE // CHUNK) for d in range(2)]
    gens = []
    for d, ck in units:
        rows = slice(ck * CHUNK, (ck + 1) * CHUNK)
        gens.append(_rw_chunk(d == 1, r[rows], v[rows], kk[rows], lw[rows, 256 * d:256 * d + 256],
                              bv[d][rows], kd[d][rows]))
    for (d, ck), (mr, ny) in zip(units, _lockstep(gens)):
        mr_out, ny_out = ((mrf, nyf), (mrb, nyb))[d]
        mr_out[0, 2 * ck * CHUNK:2 * (ck + 1) * CHUNK, :] = mr
        ny_out[0, 2 * ck * CHUNK:2 * (ck + 1) * CHUNK, :] = ny


def _rwkv_chunks(pa, rw, nct):
    b, nt_rows, _ = pa.shape
    nt = nt_rows // TILE
    hb = TILE // HALO
    nhb = nt_rows // HALO
    full = lambda w: pl.BlockSpec(w.shape, lambda bi, i: (0,) * w.ndim)
    tile = lambda rows, w: pl.BlockSpec((1, rows, w), lambda bi, i: (bi, i, 0))
    consts = (rw["mu"], rw["w0"], rw["w2"], rw["a0"], rw["a2"], rw["g2"], rw["kk"], rw["ka"], rw["rk"])
    pair = lambda dt: jax.ShapeDtypeStruct((b, 2 * nt_rows, 256), dt)
    return pl.pallas_call(
        functools.partial(_rw_chunk_kernel, nct, nt),
        grid=(b, nt),
        in_specs=[tile(TILE, RW_IN),
                  pl.BlockSpec((1, HALO, RW_IN), lambda bi, i: (bi, jnp.maximum(i * hb - 1, 0), 0)),
                  pl.BlockSpec((1, HALO, RW_IN), lambda bi, i: (bi, jnp.minimum(i * hb + hb, nhb - 1), 0))]
                 + [full(w) for w in consts],
        out_specs=[tile(2 * TILE, 256)] * 4 + [tile(TILE, 256)] * 2,
        out_shape=[pair(BF16), pair(F32), pair(BF16), pair(F32),
                   jax.ShapeDtypeStruct((b, nt_rows, 256), F32), jax.ShapeDtypeStruct((b, nt_rows, 256), F32)],
        compiler_params=_cparams("parallel", "parallel"),
    )(pa, pa, pa, *consts)


def _gla_chunk(rev, pc, a2, ab):
    n = 4 * CHUNK
    tri, _, incl_row, _ = _chunk_masks(rev)
    q = pc[:, 0:128] * (GLA_DK ** -0.5)
    k = pc[:, 128:256]
    v = pc[:, 256:512]
    lg = _log_sigmoid(_bdot(pc[:, 768:896], a2) + ab) / GLA_TAU
    yield
    bcum = _dot_sel(tri, lg)
    yield
    btot = jnp.sum(lg, axis=0, keepdims=True)
    q_in = q * jnp.exp(bcum)
    k_in = k * jnp.exp(-bcum)
    k_end = k * jnp.exp(btot - bcum)
    dec = jnp.exp(btot)
    att = jnp.where(incl_row, _bdot_nt(q_in, _big(k_in, GLA_DK)), 0.0)
    same_head = _div(_iota((n, GLA_KW), 0), 64) == _div(_iota((n, GLA_KW), 1), GLA_DK)
    inc = _collapse(jnp.where(same_head, _bdot_tn(v, k_end), 0.0))
    yield
    o_intra = _bdot(att, _big(v, 64))
    return q_in.astype(BF16), o_intra, inc, jnp.broadcast_to(dec, (8, GLA_KW))


def _state_kernel(bg, mrf, nyf, mrb, nyb, qf, oif, gnf, dcf, qb, oib, gnb, dcb, yf, yb, of, ob, srw, sgl):
    @pl.when(pl.program_id(1) == 0)
    def _():
        srw[...] = jnp.zeros_like(srw)
        sgl[...] = jnp.zeros_like(sgl)

    for d, (mr, ny, y_out) in enumerate(((mrf, nyf, yf), (mrb, nyb, yb))):
        for j in range(bg):
            s_big = _big(srw[d, j], RW_HD).astype(BF16)
            res = jnp.dot(mr[j], s_big, preferred_element_type=F32) + ny[j]
            srw[d, j] = res[0:CHUNK]
            y_out[j] = res[CHUNK:]
    for d, (q, oi, gn, dc, o_out) in enumerate(((qf, oif, gnf, dcf, of), (qb, oib, gnb, dcb, ob))):
        for j in range(bg):
            st = sgl[d, j]
            o_out[j] = oi[j] + _bdot_nt(q[j], _big(st, GLA_DK))
            sgl[d, j] = st * dc[j, 0:1, :] + gn[j]


def _scan_order(ncc, nc):
    fwd = lambda s: s
    bwd = lambda s: jnp.where(s < ncc, ncc - 1 - s, nc - 1 - (s - ncc))
    return fwd, bwd


def _state_pass(rw_terms, gla_terms, ncc):
    mrf, nyf, mrb, nyb = rw_terms
    b = mrf.shape[0]
    bg = math.gcd(b, STATE_BATCH)
    nt_rows = mrf.shape[1] // 2
    nc = nt_rows // CHUNK
    fwd, bwd = _scan_order(ncc, nc)
    blk = lambda rows, w, order: pl.BlockSpec((bg, rows, w), lambda bi, s: (bi, order(s), 0))
    rw_in = lambda order: [blk(2 * CHUNK, 256, order)] * 2
    gla_in = lambda order: [blk(CHUNK, GLA_KW, order), blk(CHUNK, 256, order), blk(CHUNK, GLA_KW, order),
                            blk(8, GLA_KW, order)]
    out = lambda order: blk(CHUNK, 256, order)
    return pl.pallas_call(
        functools.partial(_state_kernel, bg),
        grid=(b // bg, nc),
        in_specs=rw_in(fwd) + rw_in(bwd) + gla_in(fwd) + gla_in(bwd),
        out_specs=[out(fwd), out(bwd), out(fwd), out(bwd)],
        out_shape=[jax.ShapeDtypeStruct((b, nt_rows, 256), F32)] * 4,
        scratch_shapes=[pltpu.VMEM((2, bg, CHUNK, 256), F32), pltpu.VMEM((2, bg, CHUNK, GLA_KW), F32)],
        compiler_params=_cparams("parallel", "arbitrary"),
    )(mrf, nyf, mrb, nyb, *gla_terms)


def _norm_rope(x, seg, gain, cos, sin, half):
    w = x.shape[1]
    xn = x * lax.rsqrt(_sel_dot(x * x, _seg_matrix(w, seg, 1.0 / seg)) + NORM_EPS) * gain
    lane = _mod(_iota(x.shape, 1), 2 * half)
    partner = jnp.where(lane < half, pltpu.roll(xn, w - half, 1), pltpu.roll(xn, half, 1))
    return xn * cos + partner * sin


def _qk_outputs(pb, pd, cda, sda, cgq, sgq, gda, ggq, daqt, dak, davt, gqqt, gqk, gqvt):
    q = _norm_rope(pb[:, 0:256], DA_HD, gda[0:1], cda, sda, DA_HD // 2)
    daqt[0] = (q * (DA_HD ** -0.5 * LOG2E)).T.astype(BF16)
    dak[0] = _norm_rope(pb[:, 256:512], DA_HD, gda[1:2], cda, sda, DA_HD // 2).astype(BF16)
    davt[0] = pb[:, 512:768].T.astype(BF16)
    q = _norm_rope(pd[:, 0:256], GQA_HD, ggq[0:1], cgq, sgq, GQA_HD // 2)
    gqqt[0] = (q * (GQA_HD ** -0.5 * LOG2E)).T.astype(BF16)
    gqk[0] = _norm_rope(pd[:, 256:384], GQA_HD, ggq[1:2, 0:128], cgq[:, 0:128], sgq[:, 0:128],
                        GQA_HD // 2).astype(BF16)
    gqvt[0] = pd[:, 384:512].T.astype(BF16)


def _inproj_kernel(x_ref, mod_ref, g_ref, wa, wb, wc, wd, cda, sda, cgq, sgq, gda, ggq, a2_ref, ab_ref,
                   pa, sr, daqt, dak, davt, gqqt, gqk, gqvt, qf, of, nf, df, qb, ob, nb, db):
    m = mod_ref[0, 0]
    h = _modulate(x_ref[0], g_ref[...], m[0:1], m[1:2]).astype(BF16)
    pb = jnp.dot(h, wb[...], preferred_element_type=F32)
    pd = jnp.dot(h, wd[...], preferred_element_type=F32)
    pc = jnp.dot(h, wc[...], preferred_element_type=F32)
    _qk_outputs(pb, pd, cda[...], sda[...], cgq[...], sgq[...], gda[...], ggq[...],
                daqt, dak, davt, gqqt, gqk, gqvt)
    pa[0] = jnp.dot(h, wa[...], preferred_element_type=F32)
    r = pc[:, 512:768]
    sr[0] = r * _sigmoid(r)
    units = [(d, ck) for ck in range(TILE // CHUNK) for d in range(2)]
    gens = [_gla_chunk(d == 1, pc[ck * CHUNK:(ck + 1) * CHUNK], a2_ref[d], ab_ref[d]) for d, ck in units]
    for (d, ck), res in zip(units, _lockstep(gens)):
        outs = ((qf, of, nf, df), (qb, ob, nb, db))[d]
        for ref, val in zip(outs[:3], res[:3]):
            ref[0, ck * CHUNK:(ck + 1) * CHUNK, :] = val
        outs[3][0, 8 * ck:8 * ck + 8, :] = res[3]


def _in_projection(xa, modtab, g, wa, wb, wc, wd, tables, gains, a2pad, ab, nct):
    b, nt_rows, d = xa.shape
    nt = nt_rows // TILE
    full = lambda w: pl.BlockSpec(w.shape, lambda bi, i: (0,) * w.ndim)
    tile = lambda rows, w: pl.BlockSpec((1, rows, w), lambda bi, i: (bi, i, 0))
    tab = pl.BlockSpec((TILE, 256), lambda bi, i: (i, 0))
    tr = lambda w: pl.BlockSpec((1, w, TILE), lambda bi, i: (bi, 0, i))
    f32 = lambda rows, w: jax.ShapeDtypeStruct((b, rows, w), F32)
    rowmajor = lambda w: jax.ShapeDtypeStruct((b, nt_rows, w), BF16)
    transposed = lambda w: jax.ShapeDtypeStruct((b, w, nt_rows), BF16)
    gla_specs = [tile(TILE, GLA_KW), tile(TILE, 256), tile(TILE, GLA_KW), tile(TILE // CHUNK * 8, GLA_KW)]
    gla_shapes = [rowmajor(GLA_KW), f32(nt_rows, 256), f32(nt_rows, GLA_KW), f32(nt_rows // CHUNK * 8, GLA_KW)]
    return pl.pallas_call(
        _inproj_kernel,
        grid=(b, nt),
        in_specs=[tile(TILE, d),
                  pl.BlockSpec((1, 1, 6, d), lambda bi, i: (bi, _tile_type(i, nct), 0, 0)),
                  pl.BlockSpec((1, d), lambda bi, i: (0, 0)),
                  full(wa), full(wb), full(wc), full(wd), tab, tab, tab, tab,
                  full(gains[0]), full(gains[1]), full(a2pad), full(ab)],
        out_specs=[tile(TILE, RW_IN), tile(TILE, 256),
                   tr(256), tile(TILE, 256), tr(256), tr(256), tile(TILE, GQA_KVW), tr(GQA_KVW)]
                  + gla_specs + gla_specs,
        out_shape=[f32(nt_rows, RW_IN), f32(nt_rows, 256),
                   transposed(256), rowmajor(256), transposed(256),
                   transposed(256), rowmajor(GQA_KVW), transposed(GQA_KVW)]
                  + gla_shapes + gla_shapes,
        compiler_params=_cparams("parallel", "parallel"),
    )(xa, modtab, g, wa, wb, wc, wd, *tables, *gains, a2pad, ab)


def _staggered(gens):
    results = [None] * len(gens)
    next(gens[0])
    for u, gen in enumerate(gens):
        if u + 1 < len(gens):
            next(gens[u + 1])
        next(gen)
        try:
            next(gen)
        except StopIteration as stop:
            results[u] = stop.value
    return results


def _softmax_pv_t(load_k, load_wq, load_vt):
    s = jnp.dot(load_k(), load_wq(), preferred_element_type=F32)
    yield
    e = jnp.exp2(s - jnp.max(s, axis=0, keepdims=True))
    l = jnp.sum(e, axis=0, keepdims=True)
    e = e.astype(BF16)
    yield
    return jnp.dot(load_vt(), e, preferred_element_type=F32) / l


def _feature_rows(qt_ref, lo, width):
    qt = qt_ref[0]
    rows = _iota(qt.shape, 0)
    keep = jnp.logical_and(rows >= lo, rows < lo + width)
    return jnp.where(keep, qt, jnp.zeros_like(qt))


def _da_heads(lam_init, qt_ref, k_ref, vt_ref, lam_ref, g_ref, nk):
    lv = lam_ref[...]
    lam = (jnp.exp(jnp.sum(lv[0:1] * lv[1:2], axis=1, keepdims=True))
           - jnp.exp(jnp.sum(lv[2:3] * lv[3:4], axis=1, keepdims=True)) + lam_init)
    gens = []
    for h in range(4):
        for m in range(2):
            lo = 64 * h + 32 * m
            gens.append(_softmax_pv_t(lambda: k_ref[0, 0:nk, :],
                                      lambda lo=lo: _feature_rows(qt_ref, lo, DA_HD),
                                      lambda h=h: vt_ref[0, 64 * h:64 * h + 64, 0:nk]))
    o = _staggered(gens)
    outs = []
    for h in range(4):
        oh = (o[2 * h] - lam * o[2 * h + 1]).T
        oh = oh * lax.rsqrt(jnp.mean(oh * oh, axis=-1, keepdims=True) + NORM_EPS) * g_ref[...]
        outs.append(oh * (1.0 - lam_init))
    return jnp.concatenate(outs, axis=1)


def _gqa_heads(qt_ref, k_ref, vt_ref, nk):
    def wq(h):
        qh = qt_ref[0, 64 * h:64 * h + 64, :]
        zero = jnp.zeros_like(qh)
        return jnp.concatenate([qh, zero] if h // 2 == 0 else [zero, qh], axis=0)

    gens = [_softmax_pv_t(lambda: k_ref[0, 0:nk, :], lambda h=h: wq(h),
                          lambda g=h // 2: vt_ref[0, 64 * g:64 * g + 64, 0:nk]) for h in range(4)]
    return jnp.concatenate([o.T for o in _staggered(gens)], axis=1)


def _attn_kernel(heads_fn, q_tile0, nct, ctx_len, nk_all, qt_ref, k_ref, vt_ref, *rest):
    *extra, o_ref = rest
    i = pl.program_id(1) + q_tile0

    @pl.when(i < nct)
    def _():
        o_ref[0] = heads_fn(qt_ref, k_ref, vt_ref, *extra, ctx_len)

    @pl.when(i >= nct)
    def _():
        o_ref[0] = heads_fn(qt_ref, k_ref, vt_ref, *extra, nk_all)


def _attention(heads_fn, qt, k, vt, extra, q_tile0, nct):
    b, _, nt_rows = qt.shape
    nqt = nt_rows // TILE - q_tile0
    full = lambda w: pl.BlockSpec(w.shape, lambda bi, i: (0,) * w.ndim)
    return pl.pallas_call(
        functools.partial(_attn_kernel, heads_fn, q_tile0, nct, nct * TILE, nt_rows),
        grid=(b, nqt),
        in_specs=[pl.BlockSpec((1, 256, TILE), lambda bi, i: (bi, 0, i + q_tile0)),
                  pl.BlockSpec((1,) + k.shape[1:], lambda bi, i: (bi, 0, 0)),
                  pl.BlockSpec((1,) + vt.shape[1:], lambda bi, i: (bi, 0, 0))]
                 + [full(w) for w in extra],
        out_specs=pl.BlockSpec((1, TILE, 256), lambda bi, i: (bi, i, 0)),
        out_shape=jax.ShapeDtypeStruct((b, nqt * TILE, 256), F32),
        compiler_params=_cparams("parallel", "parallel"),
    )(qt, k, vt, *extra)


def _outproj_kernel(x_ref, mod_ref, yf, yb, bonus, gate, lng, lnb, ob, of, obk, sr, glag, od, w_ref, o_ref):
    m = mod_ref[0, 0]
    seg = _seg_matrix(GROUP_W, 64, 1.0 / 64)
    y = yf[0] + yb[0]
    mean = _sel_dot(y, seg)
    yc = y - mean
    var = _sel_dot(yc * yc, seg)
    oa = ((yc * lax.rsqrt(var + RW_LN_EPS)) * lng[...] + lnb[...] + bonus[0]) * gate[0]
    o = of[0] + obk[0]
    oc = (o * lax.rsqrt(_sel_dot(o * o, seg) + NORM_EPS) * glag[...]) * sr[0]
    w = w_ref[...]
    mix = (_bdot(oa, w[0:256]) + _bdot(ob[0], w[256:512]) + _bdot(oc, w[512:768]) + _bdot(od[0], w[768:1024]))
    o_ref[0] = x_ref[0] + m[2:3] * mix


def _out_projection(xa, modtab, yf, yb, bonus, gate, lng, lnb, ob, of, obk, sr, glag, od, w, t0, nct):
    b, nt_rows, d = xa.shape
    n = nt_rows // TILE - t0
    tile = lambda w_, off: pl.BlockSpec((1, TILE, w_), lambda bi, i: (bi, i + off, 0))
    row = pl.BlockSpec((1, 256), lambda bi, i: (0, 0))
    return pl.pallas_call(
        _outproj_kernel,
        grid=(b, n),
        in_specs=[tile(d, t0),
                  pl.BlockSpec((1, 1, 6, d), lambda bi, i: (bi, _tile_type(i + t0, nct), 0, 0)),
                  tile(256, t0), tile(256, t0), tile(256, t0), tile(256, t0), row, row,
                  tile(256, 0), tile(256, t0), tile(256, t0),
                  tile(256, t0), row,
                  tile(256, 0),
                  pl.BlockSpec(w.shape, lambda bi, i: (0, 0))],
        out_specs=pl.BlockSpec((1, TILE, d), lambda bi, i: (bi, i, 0)),
        out_shape=jax.ShapeDtypeStruct((b, n * TILE, d), F32),
        compiler_params=_cparams("parallel", "parallel"),
    )(xa, modtab, yf, yb, bonus, gate, lng, lnb, ob, of, obk, sr, glag, od, w)


def _ffn_kernel(t0, nct, nt, x_ref, hp_ref, hn_ref, mod_ref, g_ref, wu_ref, wg_ref, cw_ref, cb_ref, wd_ref, o_ref):
    i = pl.program_id(1) + t0
    first, last = _seq_edges(i, nct, nt)
    m = mod_ref[0, 0]
    g = g_ref[...]
    x = x_ref[0]
    h = _modulate(x, g, m[3:4], m[4:5])
    hp = jnp.where(first, 0.0, _modulate(hp_ref[0], g, m[3:4], m[4:5]))
    hn = jnp.where(last, 0.0, _modulate(hn_ref[0], g, m[3:4], m[4:5]))
    hext = jnp.concatenate([hp, h, hn], axis=0).astype(BF16)
    gext = jnp.dot(hext, wg_ref[...], preferred_element_type=F32)
    n = TILE + 2 * HALO
    prev = pltpu.roll(gext, 1, 0)[HALO:HALO + TILE]
    gg = gext[HALO:HALO + TILE]
    nxt = pltpu.roll(gext, n - 1, 0)[HALO:HALO + TILE]
    u = jnp.dot(h.astype(BF16), wu_ref[...], preferred_element_type=F32)
    cw = cw_ref[...]
    gc = cw[0:1] * prev + cw[1:2] * gg + cw[2:3] * nxt + cb_ref[...]
    act = (gc * _sigmoid(gc)) * u
    o_ref[0] = x + m[5:6] * _bdot(act, wd_ref[...])


def _ffn(xa, modtab, g, wu, wg, cw, cb, wd, t0, nct, nt):
    b, rows, d = xa.shape
    n = rows // TILE
    hb = TILE // HALO
    nhb = rows // HALO
    const = lambda w: pl.BlockSpec(w.shape, lambda bi, i: (0, 0), pipeline_mode=pl.Buffered(1))
    return pl.pallas_call(
        functools.partial(_ffn_kernel, t0, nct, nt),
        grid=(b, n),
        in_specs=[pl.BlockSpec((1, TILE, d), lambda bi, i: (bi, i, 0)),
                  pl.BlockSpec((1, HALO, d), lambda bi, i: (bi, jnp.maximum(i * hb - 1, 0), 0)),
                  pl.BlockSpec((1, HALO, d), lambda bi, i: (bi, jnp.minimum(i * hb + hb, nhb - 1), 0)),
                  pl.BlockSpec((1, 1, 6, d), lambda bi, i: (bi, _tile_type(i + t0, nct), 0, 0)),
                  pl.BlockSpec((1, d), lambda bi, i: (0, 0)),
                  const(wu), const(wg),
                  pl.BlockSpec(cw.shape, lambda bi, i: (0, 0)),
                  pl.BlockSpec(cb.shape, lambda bi, i: (0, 0)),
                  const(wd)],
        out_specs=pl.BlockSpec((1, TILE, d), lambda bi, i: (bi, i, 0)),
        out_shape=jax.ShapeDtypeStruct((b, rows, d), F32),
        compiler_params=_cparams("parallel", "parallel"),
    )(xa, xa, xa, modtab, g, wu, wg, cw, cb, wd)


def _rope_tables(ctx_len, rows, head_dim):
    row = jnp.repeat(jnp.arange(rows, dtype=F32), GRID_W)
    col = jnp.tile(jnp.arange(GRID_W, dtype=F32), rows)
    n_freq = head_dim // 4
    inv = ROPE_THETA ** (-jnp.arange(n_freq, dtype=F32) / n_freq)
    ang = jnp.concatenate([row[:, None] * inv, col[:, None] * inv], axis=-1)
    cos = jnp.concatenate([jnp.cos(ang), jnp.cos(ang)], axis=-1)
    sin = jnp.concatenate([-jnp.sin(ang), jnp.sin(ang)], axis=-1)
    reps = 256 // head_dim
    cos = jnp.tile(cos, (1, reps))
    sin = jnp.tile(sin, (1, reps))
    cos = jnp.concatenate([jnp.ones((ctx_len, 256), F32), cos], axis=0)
    sin = jnp.concatenate([jnp.zeros((ctx_len, 256), F32), sin], axis=0)
    return cos, sin


def _block_diag2(m0, m1):
    z01 = jnp.zeros((m0.shape[0], m1.shape[1]), m0.dtype)
    z10 = jnp.zeros((m1.shape[0], m0.shape[1]), m0.dtype)
    return jnp.concatenate([jnp.concatenate([m0, z01], axis=1), jnp.concatenate([z10, m1], axis=1)], axis=0)


def _gla_in_weights(w):
    d = w.shape[0]
    q, k, v, gf, gb, r = jnp.split(w, [128, 256, 512, 528, 544], axis=1)
    return jnp.concatenate([q, k, v, r, gf, gb, jnp.zeros((d, GLA_IN_PAD - GLA_IN), w.dtype)], axis=1)


def kernel(x, c, ctx, c_ctx, mod_w, mod_b, norm_mix_g, norm_ffn_g, w_in, w_out, rw_mu, rw_w0, rw_w2, rw_a0,
           rw_a2, rw_g2, rw_kk, rw_ka, rw_rk, rw_ln_g, rw_ln_b, da_qk_g, da_lam, da_subln_g, gla_a2, gla_ab,
           gla_norm_g, gqa_qk_g, ffn_w_up, ffn_conv_w, ffn_conv_b, ffn_w_down):
    b, seq, d = x.shape
    ctx_len = ctx.shape[1]
    depth = mod_w.shape[0]
    assert d == D_MODEL and seq % TILE == 0 and ctx_len % TILE == 0 and seq % GRID_W == 0
    nt_rows = ctx_len + seq
    nt = nt_rows // TILE
    nct = ctx_len // TILE
    ncc = ctx_len // CHUNK

    rows = -(-(b + 1) // 8) * 8
    cc = jnp.concatenate([c, c_ctx[None, :], jnp.zeros((rows - b - 1, d), F32)], axis=0)
    mods = _modulation(cc, mod_w, mod_b).reshape(depth, rows, 6, d)

    cda, sda = _rope_tables(ctx_len, seq // GRID_W, DA_HD)
    cgq, sgq = _rope_tables(ctx_len, seq // GRID_W, GQA_HD)

    xa = jnp.concatenate([ctx, x], axis=1)
    for i in range(depth):
        last = i == depth - 1
        t0 = nct if last else 0
        modtab = jnp.stack([jnp.broadcast_to(mods[i, b], (b, 6, d)), mods[i, :b]], axis=1)
        row = lambda v: v.reshape(1, -1)

        wi = w_in[i]
        wa = wi[:, 0:RW_IN].astype(BF16)
        wb = wi[:, RW_IN:RW_IN + DA_IN].astype(BF16)
        wc = _gla_in_weights(wi[:, RW_IN + DA_IN:RW_IN + DA_IN + GLA_IN]).astype(BF16)
        wd = wi[:, RW_IN + DA_IN + GLA_IN:].astype(BF16)
        a2pad = jnp.zeros((2, 128, GLA_KW), F32)
        a2pad = a2pad.at[0, 0:16].set(gla_a2[i, 0]).at[1, 16:32].set(gla_a2[i, 1])
        gains = (jnp.tile(da_qk_g[i], (1, 256 // DA_HD)), jnp.tile(gqa_qk_g[i], (1, 256 // GQA_HD)))
        pa, sr, daqt, dak, davt, gqqt, gqk, gqvt, *gla_terms = _in_projection(
            xa, modtab, row(norm_mix_g[i]), wa, wb, wc, wd, (cda, sda, cgq, sgq), gains, a2pad,
            gla_ab[i].reshape(2, 1, GLA_KW), nct)

        rw = dict(mu=rw_mu[i], w0=row(rw_w0[i]), w2=_block_diag2(rw_w2[i, 0], rw_w2[i, 1]),
                  a0=row(rw_a0[i]), a2=_block_diag2(rw_a2[i, 0], rw_a2[i, 1]), g2=rw_g2[i],
                  kk=row(rw_kk[i]), ka=row(rw_ka[i]), rk=row(rw_rk[i]))
        mrf, nyf, mrb, nyb, bonus, gate = _rwkv_chunks(pa, rw, nct)
        yf, yb, of, obk = _state_pass((mrf, nyf, mrb, nyb), gla_terms, ncc)

        lam_init = 0.8 - 0.6 * math.exp(-0.3 * i)
        ob = _attention(functools.partial(_da_heads, lam_init), daqt, dak, davt,
                        (da_lam[i], row(da_subln_g[i])), t0, nct)
        od = _attention(_gqa_heads, gqqt, gqk, gqvt, (), t0, nct)

        xa = _out_projection(xa, modtab, yf, yb, bonus, gate, row(rw_ln_g[i]), row(rw_ln_b[i]), ob, of, obk, sr,
                             row(jnp.tile(gla_norm_g[i], 4)), od, w_out[i].astype(BF16), t0, nct)
        wu = ffn_w_up[i]
        xa = _ffn(xa, modtab, row(norm_ffn_g[i]), wu[:, :D_FF].astype(BF16), wu[:, D_FF:].astype(BF16),
                  ffn_conv_w[i], row(ffn_conv_b[i]), ffn_w_down[i].astype(BF16), t0, nct, nt)
    return xa
```

```python
import functools
import math

import jax
import jax.numpy as jnp
from jax import lax
from jax.experimental import pallas as pl
from jax.experimental.pallas import tpu as pltpu

F32 = jnp.float32
BF16 = jnp.bfloat16
LOG2E = math.log2(math.e)

D_MODEL = 1024
GRID_W = 64
GROUP_W = 256
NORM_EPS = 1e-6
ROPE_THETA = 10000.0
RW_HD = 64
RW_IN = 1152
RW_LN_EPS = 64e-5
DA_HD = 32
DA_IN = 768
GLA_DK = 32
GLA_KW = 128
GLA_TAU = 16.0
GLA_IN = 800
GLA_IN_PAD = 896
GQA_HD = 64
GQA_KVW = 128
GQA_IN = 512
D_FF = 2816

CHUNK = 64
TILE = 256
HALO = 8
STATE_BATCH = 8
ATTN_GROUP = 4
VMEM_LIMIT = 56 * 1024 * 1024


def _cparams(*sem):
    return pltpu.CompilerParams(dimension_semantics=sem, vmem_limit_bytes=VMEM_LIMIT)


def _bdot(a, b):
    return jnp.dot(a.astype(BF16), b.astype(BF16), preferred_element_type=F32)


def _bdot_nt(a, b):
    return lax.dot_general(a.astype(BF16), b.astype(BF16), (((1,), (1,)), ((), ())),
                           preferred_element_type=F32)


def _bdot_tn(a, b):
    return jnp.dot(a.astype(F32).T.astype(BF16), b.astype(BF16), preferred_element_type=F32)


def _bf16_pieces(a):
    hi = a.astype(BF16)
    rest = a - hi.astype(F32)
    mid = rest.astype(BF16)
    lo = (rest - mid.astype(F32)).astype(BF16)
    return hi, mid, lo


def _sel_dot(a, sel):
    n = a.shape[0]
    out = jnp.dot(jnp.concatenate(_bf16_pieces(a), axis=0), sel.astype(BF16), preferred_element_type=F32)
    return out[0:n] + out[n:2 * n] + out[2 * n:]


def _dot_sel(sel, a):
    n = a.shape[1]
    out = jnp.dot(sel.astype(BF16), jnp.concatenate(_bf16_pieces(a), axis=1), preferred_element_type=F32)
    return out[:, 0:n] + out[:, n:2 * n] + out[:, 2 * n:]


def _iota(shape, dim):
    return lax.broadcasted_iota(jnp.int32, shape, dim)


def _div(x, n):
    return x >> (n.bit_length() - 1)


def _mod(x, n):
    return x & (n - 1)


def _sigmoid(x):
    return 1.0 / (1.0 + jnp.exp(-x))


def _log_sigmoid(x):
    return jnp.minimum(x, 0.0) - jnp.log(1.0 + jnp.exp(-jnp.abs(x)))


def _seg_matrix(width, seg, value):
    r = _div(_iota((width, width), 0), seg)
    c = _div(_iota((width, width), 1), seg)
    return jnp.where(r == c, value, 0.0).astype(F32)


def _modulate(x, g, shift, scale):
    y = x * lax.rsqrt(jnp.mean(x * x, axis=-1, keepdims=True) + NORM_EPS)
    return (y * g) * (1.0 + scale) + shift


def _shift_rows(x, hp_row, hn_row):
    n = x.shape[0]
    rows = _iota(x.shape, 0)
    prev = jnp.where(rows == 0, hp_row, pltpu.roll(x, 1, 0))
    nxt = jnp.where(rows == n - 1, hn_row, pltpu.roll(x, n - 1, 0))
    return prev, nxt


def _tile_type(i, nct):
    return jnp.where(i >= nct, 1, 0)


def _seq_edges(i, nct, nt):
    first = jnp.logical_or(i == 0, i == nct)
    last = jnp.logical_or(i == nct - 1, i == nt - 1)
    return first, last


def _mod_kernel(c_ref, w_ref, b_ref, o_ref):
    cc = c_ref[...]
    s = cc * _sigmoid(cc)
    o_ref[0] = _bdot(s, w_ref[0]) + b_ref[0]


def _modulation(cc, mod_w, mod_b):
    depth, d, n = mod_w.shape
    tn = 1536
    rows = cc.shape[0]
    return pl.pallas_call(
        _mod_kernel,
        grid=(depth, n // tn),
        in_specs=[pl.BlockSpec((rows, d), lambda l, j: (0, 0)),
                  pl.BlockSpec((1, d, tn), lambda l, j: (l, 0, j)),
                  pl.BlockSpec((1, 1, tn), lambda l, j: (l, 0, j))],
        out_specs=pl.BlockSpec((1, rows, tn), lambda l, j: (l, 0, j)),
        out_shape=jax.ShapeDtypeStruct((depth, rows, n), F32),
        compiler_params=_cparams("parallel", "parallel"),
    )(cc, mod_w, mod_b.reshape(depth, 1, n))


def _big(x, width_seg):
    t = jnp.concatenate([x, x, x, x], axis=0)
    rh = _div(_iota(t.shape, 0), CHUNK)
    ch = _div(_iota(t.shape, 1), width_seg)
    return jnp.where(rh == ch, t, 0.0)


def _collapse(x):
    return x[0:CHUNK] + x[CHUNK:2 * CHUNK] + x[2 * CHUNK:3 * CHUNK] + x[3 * CHUNK:4 * CHUNK]


def _chunk_masks(rev):
    n = 4 * CHUNK
    ri = _iota((CHUNK, CHUNK), 0)
    ci = _iota((CHUNK, CHUNK), 1)
    tri = jnp.where((ci >= ri) if rev else (ci <= ri), 1.0, 0.0).astype(F32)
    rt_row = _iota((CHUNK, n), 0)
    ct_row = _mod(_iota((CHUNK, n), 1), CHUNK)
    strict_row = (ct_row > rt_row) if rev else (ct_row < rt_row)
    incl_row = (ct_row >= rt_row) if rev else (ct_row <= rt_row)
    eye_row = ct_row == rt_row
    return tri, strict_row, incl_row, eye_row


def _lockstep(gens):
    results = [None] * len(gens)
    active = list(range(len(gens)))
    while active:
        for idx in list(active):
            try:
                next(gens[idx])
            except StopIteration as stop:
                results[idx] = stop.value
                active.remove(idx)
    return results


def _rw_chunk(rev, r, v, kk, lw, bv, kd):
    n = 4 * CHUNK
    tri, strict_row, incl_row, eye_row = _chunk_masks(rev)
    same_head = _div(_iota((n, n), 0), CHUNK) == _div(_iota((n, n), 1), CHUNK)
    c = _dot_sel(tri, lw)
    yield
    ctot = jnp.sum(lw, axis=0, keepdims=True)
    at = -kk * jnp.exp(c - lw)
    rt = r * jnp.exp(c)
    eneg = jnp.exp(-c)
    bt = bv * eneg
    kt = kd * eneg
    eend = jnp.exp(ctot - c)
    bh = bv * eend
    kh = kd * eend
    dend = jnp.exp(ctot)
    at_big = _big(at, RW_HD).astype(BF16)
    v_big = _big(v, RW_HD).astype(BF16)
    lhs = jnp.concatenate([at, rt], axis=0)
    gb = _bdot_nt(lhs, _big(bt, RW_HD))
    gk = _bdot_nt(lhs, _big(kt, RW_HD))
    yield
    nrow = jnp.where(strict_row, gb[0:CHUNK], 0.0)
    a_rb = jnp.where(incl_row, gb[CHUNK:], 0.0)
    a_ak = jnp.where(strict_row, gk[0:CHUNK], 0.0)
    a_rk = jnp.where(incl_row, gk[CHUNK:], 0.0)
    t_row = jnp.where(eye_row, 1.0, 0.0) + nrow
    q_row = _bdot(nrow, _big(nrow, CHUNK))
    wk = _bdot(jnp.concatenate([a_ak, a_rk], axis=0), v_big)
    yield
    for level in range(5):
        q_big = _big(q_row, CHUNK).astype(BF16)
        if level < 4:
            res = _bdot(jnp.concatenate([t_row, q_row], axis=0), q_big)
            t_row = t_row + res[0:CHUNK]
            q_row = res[CHUNK:]
        else:
            t_row = t_row + _bdot(t_row, q_big)
        yield
    ah = _bdot(t_row, at_big)
    uv = _bdot(t_row, _big(wk[0:CHUNK], RW_HD))
    yield
    rh = rt + _bdot(a_rb, _big(ah, RW_HD))
    yh = _bdot(a_rb, _big(uv, RW_HD)) + wk[CHUNK:]
    m_row = jnp.where(eye_row, dend, 0.0) + _collapse(jnp.where(same_head, _bdot_tn(bh, ah), 0.0))
    n_row = _collapse(jnp.where(same_head, _bdot_tn(bh, uv) + _bdot_tn(kh, v), 0.0))
    return (jnp.concatenate([m_row, rh], axis=0).astype(BF16), jnp.concatenate([n_row, yh], axis=0))


def _rw_chunk_kernel(nct, nt, pa_ref, hp_ref, hn_ref, mu_ref, w0_ref, w2_ref, a0_ref, a2_ref,
                     g2_ref, kk_ref, ka_ref, rk_ref, mrf, nyf, mrb, nyb, bonus_out, gate_out):
    i = pl.program_id(1)
    first, last = _seq_edges(i, nct, nt)
    x = pa_ref[0]
    hp = jnp.where(first, 0.0, hp_ref[0, HALO - 1:HALO, :])
    hn = jnp.where(last, 0.0, hn_ref[0, 0:1, :])
    prev, nxt = _shift_rows(x, hp, hn)
    mu = mu_ref[...]
    xm = x + mu[0:1] * (prev - x) + mu[1:2] * (nxt - x)
    r = xm[:, 0:256]
    k = xm[:, 256:512]
    v = xm[:, 512:768]
    wfb = xm[:, 768:896]
    afb = xm[:, 896:1024]
    g = xm[:, 1024:1152]
    seg = _seg_matrix(GROUP_W, RW_HD, 1.0)
    gate_out[0] = _bdot(_sigmoid(g), g2_ref[...])
    kk = k * kk_ref[...]
    kk = kk * lax.rsqrt(_sel_dot(kk * kk, seg) + 1e-12)
    wl = w0_ref[...] + _bdot(jnp.tanh(wfb), w2_ref[...])
    lw = -jnp.exp(_log_sigmoid(wl) - 0.5)
    a = _sigmoid(a0_ref[...] + _bdot(afb, a2_ref[...]))
    ka = ka_ref[...]
    kd = [k * (1.0 + (a[:, 256 * d:256 * d + 256] - 1.0) * ka) for d in range(2)]
    bv = [kk * a[:, 256 * d:256 * d + 256] for d in range(2)]
    rk = rk_ref[...]
    bonus_out[0] = _sel_dot(r * (kd[0] + kd[1]) * rk, seg) * v
    units = [(d, ck) for ck in range(TILE // CHUNK) for d in range(2)]
    gens = []
    for d, ck in units:
        rows = slice(ck * CHUNK, (ck + 1) * CHUNK)
        gens.append(_rw_chunk(d == 1, r[rows], v[rows], kk[rows], lw[rows, 256 * d:256 * d + 256],
                              bv[d][rows], kd[d][rows]))
    for (d, ck), (mr, ny) in zip(units, _lockstep(gens)):
        mr_out, ny_out = ((mrf, nyf), (mrb, nyb))[d]
        mr_out[0, 2 * ck * CHUNK:2 * (ck + 1) * CHUNK, :] = mr
        ny_out[0, 2 * ck * CHUNK:2 * (ck + 1) * CHUNK, :] = ny


def _rwkv_chunks(pa, rw, nct):
    b, nt_rows, _ = pa.shape
    nt = nt_rows // TILE
    hb = TILE // HALO
    nhb = nt_rows // HALO
    full = lambda w: pl.BlockSpec(w.shape, lambda bi, i: (0,) * w.ndim)
    tile = lambda rows, w: pl.BlockSpec((1, rows, w), lambda bi, i: (bi, i, 0))
    consts = (rw["mu"], rw["w0"], rw["w2"], rw["a0"], rw["a2"], rw["g2"], rw["kk"], rw["ka"], rw["rk"])
    pair = lambda dt: jax.ShapeDtypeStruct((b, 2 * nt_rows, 256), dt)
    return pl.pallas_call(
        functools.partial(_rw_chunk_kernel, nct, nt),
        grid=(b, nt),
        in_specs=[tile(TILE, RW_IN),
                  pl.BlockSpec((1, HALO, RW_IN), lambda bi, i: (bi, jnp.maximum(i * hb - 1, 0), 0)),
                  pl.BlockSpec((1, HALO, RW_IN), lambda bi, i: (bi, jnp.minimum(i * hb + hb, nhb - 1), 0))]
                 + [full(w) for w in consts],
        out_specs=[tile(2 * TILE, 256)] * 4 + [tile(TILE, 256)] * 2,
        out_shape=[pair(BF16), pair(F32), pair(BF16), pair(F32),
                   jax.ShapeDtypeStruct((b, nt_rows, 256), F32), jax.ShapeDtypeStruct((b, nt_rows, 256), F32)],
        compiler_params=_cparams("parallel", "parallel"),
    )(pa, pa, pa, *consts)


def _gla_chunk(rev, pc, a2, ab):
    n = 4 * CHUNK
    tri, _, incl_row, _ = _chunk_masks(rev)
    q = pc[:, 0:128] * (GLA_DK ** -0.5)
    k = pc[:, 128:256]
    v = pc[:, 256:512]
    lg = _log_sigmoid(_bdot(pc[:, 768:896], a2) + ab) / GLA_TAU
    yield
    bcum = _dot_sel(tri, lg)
    yield
    btot = jnp.sum(lg, axis=0, keepdims=True)
    q_in = q * jnp.exp(bcum)
    k_in = k * jnp.exp(-bcum)
    k_end = k * jnp.exp(btot - bcum)
    dec = jnp.exp(btot)
    att = jnp.where(incl_row, _bdot_nt(q_in, _big(k_in, GLA_DK)), 0.0)
    same_head = _div(_iota((n, GLA_KW), 0), 64) == _div(_iota((n, GLA_KW), 1), GLA_DK)
    inc = _collapse(jnp.where(same_head, _bdot_tn(v, k_end), 0.0))
    yield
    o_intra = _bdot(att, _big(v, 64))
    return q_in.astype(BF16), o_intra, inc, jnp.broadcast_to(dec, (8, GLA_KW))


def _state_kernel(bg, mrf, nyf, mrb, nyb, qf, oif, gnf, dcf, qb, oib, gnb, dcb, yf, yb, of, ob, srw, sgl):
    @pl.when(pl.program_id(1) == 0)
    def _():
        srw[...] = jnp.zeros_like(srw)
        sgl[...] = jnp.zeros_like(sgl)

    for d, (mr, ny, y_out) in enumerate(((mrf, nyf, yf), (mrb, nyb, yb))):
        for j in range(bg):
            s_big = _big(srw[d, j], RW_HD).astype(BF16)
            res = jnp.dot(mr[j], s_big, preferred_element_type=F32) + ny[j]
            srw[d, j] = res[0:CHUNK]
            y_out[j] = res[CHUNK:]
    for d, (q, oi, gn, dc, o_out) in enumerate(((qf, oif, gnf, dcf, of), (qb, oib, gnb, dcb, ob))):
        for j in range(bg):
            st = sgl[d, j]
            o_out[j] = oi[j] + _bdot_nt(q[j], _big(st, GLA_DK))
            sgl[d, j] = st * dc[j, 0:1, :] + gn[j]


def _scan_order(ncc, nc):
    fwd = lambda s: s
    bwd = lambda s: jnp.where(s < ncc, ncc - 1 - s, nc - 1 - (s - ncc))
    return fwd, bwd


def _state_pass(rw_terms, gla_terms, ncc):
    mrf, nyf, mrb, nyb = rw_terms
    b = mrf.shape[0]
    bg = math.gcd(b, STATE_BATCH)
    nt_rows = mrf.shape[1] // 2
    nc = nt_rows // CHUNK
    fwd, bwd = _scan_order(ncc, nc)
    blk = lambda rows, w, order: pl.BlockSpec((bg, rows, w), lambda bi, s: (bi, order(s), 0))
    rw_in = lambda order: [blk(2 * CHUNK, 256, order)] * 2
    gla_in = lambda order: [blk(CHUNK, GLA_KW, order), blk(CHUNK, 256, order), blk(CHUNK, GLA_KW, order),
                            blk(8, GLA_KW, order)]
    out = lambda order: blk(CHUNK, 256, order)
    return pl.pallas_call(
        functools.partial(_state_kernel, bg),
        grid=(b // bg, nc),
        in_specs=rw_in(fwd) + rw_in(bwd) + gla_in(fwd) + gla_in(bwd),
        out_specs=[out(fwd), out(bwd), out(fwd), out(bwd)],
        out_shape=[jax.ShapeDtypeStruct((b, nt_rows, 256), F32)] * 4,
        scratch_shapes=[pltpu.VMEM((2, bg, CHUNK, 256), F32), pltpu.VMEM((2, bg, CHUNK, GLA_KW), F32)],
        compiler_params=_cparams("parallel", "arbitrary"),
    )(mrf, nyf, mrb, nyb, *gla_terms)


def _norm_rope(x, seg, gain, cos, sin, half):
    w = x.shape[1]
    xn = x * lax.rsqrt(_sel_dot(x * x, _seg_matrix(w, seg, 1.0 / seg)) + NORM_EPS) * gain
    lane = _mod(_iota(x.shape, 1), 2 * half)
    partner = jnp.where(lane < half, pltpu.roll(xn, w - half, 1), pltpu.roll(xn, half, 1))
    return xn * cos + partner * sin


def _qk_outputs(pb, pd, cda, sda, cgq, sgq, gda, ggq, daqt, dak, davt, gqqt, gqk, gqvt):
    q = _norm_rope(pb[:, 0:256], DA_HD, gda[0:1], cda, sda, DA_HD // 2)
    daqt[0] = (q * (DA_HD ** -0.5 * LOG2E)).T.astype(BF16)
    dak[0] = _norm_rope(pb[:, 256:512], DA_HD, gda[1:2], cda, sda, DA_HD // 2).astype(BF16)
    davt[0] = pb[:, 512:768].T.astype(BF16)
    q = _norm_rope(pd[:, 0:256], GQA_HD, ggq[0:1], cgq, sgq, GQA_HD // 2)
    gqqt[0] = (q * (GQA_HD ** -0.5 * LOG2E)).T.astype(BF16)
    gqk[0] = _norm_rope(pd[:, 256:384], GQA_HD, ggq[1:2, 0:128], cgq[:, 0:128], sgq[:, 0:128],
                        GQA_HD // 2).astype(BF16)
    gqvt[0] = pd[:, 384:512].T.astype(BF16)


def _inproj_kernel(x_ref, mod_ref, g_ref, wa, wb, wc, wd, cda, sda, cgq, sgq, gda, ggq, a2_ref, ab_ref,
                   pa, sr, daqt, dak, davt, gqqt, gqk, gqvt, qf, of, nf, df, qb, ob, nb, db):
    m = mod_ref[0, 0]
    h = _modulate(x_ref[0], g_ref[...], m[0:1], m[1:2]).astype(BF16)
    pb = jnp.dot(h, wb[...], preferred_element_type=F32)
    pd = jnp.dot(h, wd[...], preferred_element_type=F32)
    pc = jnp.dot(h, wc[...], preferred_element_type=F32)
    _qk_outputs(pb, pd, cda[...], sda[...], cgq[...], sgq[...], gda[...], ggq[...],
                daqt, dak, davt, gqqt, gqk, gqvt)
    pa[0] = jnp.dot(h, wa[...], preferred_element_type=F32)
    r = pc[:, 512:768]
    sr[0] = r * _sigmoid(r)
    units = [(d, ck) for ck in range(TILE // CHUNK) for d in range(2)]
    gens = [_gla_chunk(d == 1, pc[ck * CHUNK:(ck + 1) * CHUNK], a2_ref[d], ab_ref[d]) for d, ck in units]
    for (d, ck), res in zip(units, _lockstep(gens)):
        outs = ((qf, of, nf, df), (qb, ob, nb, db))[d]
        for ref, val in zip(outs[:3], res[:3]):
            ref[0, ck * CHUNK:(ck + 1) * CHUNK, :] = val
        outs[3][0, 8 * ck:8 * ck + 8, :] = res[3]


def _in_projection(xa, modtab, g, wa, wb, wc, wd, tables, gains, a2pad, ab, nct):
    b, nt_rows, d = xa.shape
    nt = nt_rows // TILE
    full = lambda w: pl.BlockSpec(w.shape, lambda bi, i: (0,) * w.ndim)
    tile = lambda rows, w: pl.BlockSpec((1, rows, w), lambda bi, i: (bi, i, 0))
    tab = pl.BlockSpec((TILE, 256), lambda bi, i: (i, 0))
    tr = lambda w: pl.BlockSpec((1, w, TILE), lambda bi, i: (bi, 0, i))
    f32 = lambda rows, w: jax.ShapeDtypeStruct((b, rows, w), F32)
    rowmajor = lambda w: jax.ShapeDtypeStruct((b, nt_rows, w), BF16)
    transposed = lambda w: jax.ShapeDtypeStruct((b, w, nt_rows), BF16)
    gla_specs = [tile(TILE, GLA_KW), tile(TILE, 256), tile(TILE, GLA_KW), tile(TILE // CHUNK * 8, GLA_KW)]
    gla_shapes = [rowmajor(GLA_KW), f32(nt_rows, 256), f32(nt_rows, GLA_KW), f32(nt_rows // CHUNK * 8, GLA_KW)]
    return pl.pallas_call(
        _inproj_kernel,
        grid=(b, nt),
        in_specs=[tile(TILE, d),
                  pl.BlockSpec((1, 1, 6, d), lambda bi, i: (bi, _tile_type(i, nct), 0, 0)),
                  pl.BlockSpec((1, d), lambda bi, i: (0, 0)),
                  full(wa), full(wb), full(wc), full(wd), tab, tab, tab, tab,
                  full(gains[0]), full(gains[1]), full(a2pad), full(ab)],
        out_specs=[tile(TILE, RW_IN), tile(TILE, 256),
                   tr(256), tile(TILE, 256), tr(256), tr(256), tile(TILE, GQA_KVW), tr(GQA_KVW)]
                  + gla_specs + gla_specs,
        out_shape=[f32(nt_rows, RW_IN), f32(nt_rows, 256),
                   transposed(256), rowmajor(256), transposed(256),
                   transposed(256), rowmajor(GQA_KVW), transposed(GQA_KVW)]
                  + gla_shapes + gla_shapes,
        compiler_params=_cparams("parallel", "parallel"),
    )(xa, modtab, g, wa, wb, wc, wd, *tables, *gains, a2pad, ab)


def _staggered(gens):
    results = [None] * len(gens)
    groups = [list(range(g, min(g + ATTN_GROUP, len(gens)))) for g in range(0, len(gens), ATTN_GROUP)]
    for u in groups[0]:
        next(gens[u])
    for gi, group in enumerate(groups):
        if gi + 1 < len(groups):
            for u in groups[gi + 1]:
                next(gens[u])
        for u in group:
            next(gens[u])
        for u in group:
            try:
                next(gens[u])
            except StopIteration as stop:
                results[u] = stop.value
    return results


def _softmax_pv_t(load_k, load_wq, load_vt):
    s = jnp.dot(load_k(), load_wq(), preferred_element_type=F32)
    yield
    e = jnp.exp2(s - jnp.max(s, axis=0, keepdims=True))
    l = jnp.sum(e, axis=0, keepdims=True)
    e = e.astype(BF16)
    yield
    return jnp.dot(load_vt(), e, preferred_element_type=F32) / l


def _feature_rows(qt_ref, lo, width):
    qt = qt_ref[0]
    rows = _iota(qt.shape, 0)
    keep = jnp.logical_and(rows >= lo, rows < lo + width)
    return jnp.where(keep, qt, jnp.zeros_like(qt))


def _da_heads(lam_init, qt_ref, k_ref, vt_ref, lam_ref, g_ref, nk):
    lv = lam_ref[...]
    lam = (jnp.exp(jnp.sum(lv[0:1] * lv[1:2], axis=1, keepdims=True))
           - jnp.exp(jnp.sum(lv[2:3] * lv[3:4], axis=1, keepdims=True)) + lam_init)
    gens = []
    for h in range(4):
        for m in range(2):
            lo = 64 * h + 32 * m
            gens.append(_softmax_pv_t(lambda: k_ref[0, 0:nk, :],
                                      lambda lo=lo: _feature_rows(qt_ref, lo, DA_HD),
                                      lambda h=h: vt_ref[0, 64 * h:64 * h + 64, 0:nk]))
    o = _staggered(gens)
    outs = []
    for h in range(4):
        oh = (o[2 * h] - lam * o[2 * h + 1]).T
        oh = oh * lax.rsqrt(jnp.mean(oh * oh, axis=-1, keepdims=True) + NORM_EPS) * g_ref[...]
        outs.append(oh * (1.0 - lam_init))
    return jnp.concatenate(outs, axis=1)


def _gqa_heads(qt_ref, k_ref, vt_ref, nk):
    def wq(h):
        qh = qt_ref[0, 64 * h:64 * h + 64, :]
        zero = jnp.zeros_like(qh)
        return jnp.concatenate([qh, zero] if h // 2 == 0 else [zero, qh], axis=0)

    gens = [_softmax_pv_t(lambda: k_ref[0, 0:nk, :], lambda h=h: wq(h),
                          lambda g=h // 2: vt_ref[0, 64 * g:64 * g + 64, 0:nk]) for h in range(4)]
    return jnp.concatenate([o.T for o in _staggered(gens)], axis=1)


def _attn_kernel(heads_fn, q_tile0, nct, ctx_len, nk_all, qt_ref, k_ref, vt_ref, *rest):
    *extra, o_ref = rest
    i = pl.program_id(1) + q_tile0

    @pl.when(i < nct)
    def _():
        o_ref[0] = heads_fn(qt_ref, k_ref, vt_ref, *extra, ctx_len).astype(BF16)

    @pl.when(i >= nct)
    def _():
        o_ref[0] = heads_fn(qt_ref, k_ref, vt_ref, *extra, nk_all).astype(BF16)


def _attention(heads_fn, qt, k, vt, extra, q_tile0, nct):
    b, _, nt_rows = qt.shape
    nqt = nt_rows // TILE - q_tile0
    full = lambda w: pl.BlockSpec(w.shape, lambda bi, i: (0,) * w.ndim)
    return pl.pallas_call(
        functools.partial(_attn_kernel, heads_fn, q_tile0, nct, nct * TILE, nt_rows),
        grid=(b, nqt),
        in_specs=[pl.BlockSpec((1, 256, TILE), lambda bi, i: (bi, 0, i + q_tile0)),
                  pl.BlockSpec((1,) + k.shape[1:], lambda bi, i: (bi, 0, 0)),
                  pl.BlockSpec((1,) + vt.shape[1:], lambda bi, i: (bi, 0, 0))]
                 + [full(w) for w in extra],
        out_specs=pl.BlockSpec((1, TILE, 256), lambda bi, i: (bi, i, 0)),
        out_shape=jax.ShapeDtypeStruct((b, nqt * TILE, 256), BF16),
        compiler_params=_cparams("parallel", "parallel"),
    )(qt, k, vt, *extra)


def _outproj_kernel(x_ref, mod_ref, yf, yb, bonus, gate, lng, lnb, ob, of, obk, sr, glag, od, w_ref, o_ref):
    m = mod_ref[0, 0]
    seg = _seg_matrix(GROUP_W, 64, 1.0 / 64)
    y = yf[0] + yb[0]
    mean = _sel_dot(y, seg)
    yc = y - mean
    var = _sel_dot(yc * yc, seg)
    oa = ((yc * lax.rsqrt(var + RW_LN_EPS)) * lng[...] + lnb[...] + bonus[0]) * gate[0]
    o = of[0] + obk[0]
    oc = (o * lax.rsqrt(_sel_dot(o * o, seg) + NORM_EPS) * glag[...]) * sr[0]
    w = w_ref[...]
    mix = (_bdot(oa, w[0:256]) + _bdot(ob[0], w[256:512]) + _bdot(oc, w[512:768]) + _bdot(od[0], w[768:1024]))
    o_ref[0] = x_ref[0] + m[2:3] * mix


def _out_projection(xa, modtab, yf, yb, bonus, gate, lng, lnb, ob, of, obk, sr, glag, od, w, t0, nct):
    b, nt_rows, d = xa.shape
    n = nt_rows // TILE - t0
    tile = lambda w_, off: pl.BlockSpec((1, TILE, w_), lambda bi, i: (bi, i + off, 0))
    row = pl.BlockSpec((1, 256), lambda bi, i: (0, 0))
    return pl.pallas_call(
        _outproj_kernel,
        grid=(b, n),
        in_specs=[tile(d, t0),
                  pl.BlockSpec((1, 1, 6, d), lambda bi, i: (bi, _tile_type(i + t0, nct), 0, 0)),
                  tile(256, t0), tile(256, t0), tile(256, t0), tile(256, t0), row, row,
                  tile(256, 0), tile(256, t0), tile(256, t0),
                  tile(256, t0), row,
                  tile(256, 0),
                  pl.BlockSpec(w.shape, lambda bi, i: (0, 0))],
        out_specs=pl.BlockSpec((1, TILE, d), lambda bi, i: (bi, i, 0)),
        out_shape=jax.ShapeDtypeStruct((b, n * TILE, d), F32),
        compiler_params=_cparams("parallel", "parallel"),
    )(xa, modtab, yf, yb, bonus, gate, lng, lnb, ob, of, obk, sr, glag, od, w)


def _ffn_kernel(t0, nct, nt, x_ref, hp_ref, hn_ref, mod_ref, g_ref, wu_ref, wg_ref, cw_ref, cb_ref, wd_ref, o_ref):
    i = pl.program_id(1) + t0
    first, last = _seq_edges(i, nct, nt)
    m = mod_ref[0, 0]
    g = g_ref[...]
    x = x_ref[0]
    h = _modulate(x, g, m[3:4], m[4:5])
    hp = jnp.where(first, 0.0, _modulate(hp_ref[0], g, m[3:4], m[4:5]))
    hn = jnp.where(last, 0.0, _modulate(hn_ref[0], g, m[3:4], m[4:5]))
    hext = jnp.concatenate([hp, h, hn], axis=0).astype(BF16)
    gext = jnp.dot(hext, wg_ref[...], preferred_element_type=F32)
    n = TILE + 2 * HALO
    prev = pltpu.roll(gext, 1, 0)[HALO:HALO + TILE]
    gg = gext[HALO:HALO + TILE]
    nxt = pltpu.roll(gext, n - 1, 0)[HALO:HALO + TILE]
    u = jnp.dot(h.astype(BF16), wu_ref[...], preferred_element_type=F32)
    cw = cw_ref[...]
    gc = cw[0:1] * prev + cw[1:2] * gg + cw[2:3] * nxt + cb_ref[...]
    act = (gc * _sigmoid(gc)) * u
    o_ref[0] = x + m[5:6] * _bdot(act, wd_ref[...])


def _ffn(xa, modtab, g, wu, wg, cw, cb, wd, t0, nct, nt):
    b, rows, d = xa.shape
    n = rows // TILE
    hb = TILE // HALO
    nhb = rows // HALO
    const = lambda w: pl.BlockSpec(w.shape, lambda bi, i: (0, 0), pipeline_mode=pl.Buffered(1))
    return pl.pallas_call(
        functools.partial(_ffn_kernel, t0, nct, nt),
        grid=(b, n),
        in_specs=[pl.BlockSpec((1, TILE, d), lambda bi, i: (bi, i, 0)),
                  pl.BlockSpec((1, HALO, d), lambda bi, i: (bi, jnp.maximum(i * hb - 1, 0), 0)),
                  pl.BlockSpec((1, HALO, d), lambda bi, i: (bi, jnp.minimum(i * hb + hb, nhb - 1), 0)),
                  pl.BlockSpec((1, 1, 6, d), lambda bi, i: (bi, _tile_type(i + t0, nct), 0, 0)),
                  pl.BlockSpec((1, d), lambda bi, i: (0, 0)),
                  const(wu), const(wg),
                  pl.BlockSpec(cw.shape, lambda bi, i: (0, 0)),
                  pl.BlockSpec(cb.shape, lambda bi, i: (0, 0)),
                  const(wd)],
        out_specs=pl.BlockSpec((1, TILE, d), lambda bi, i: (bi, i, 0)),
        out_shape=jax.ShapeDtypeStruct((b, rows, d), F32),
        compiler_params=_cparams("parallel", "parallel"),
    )(xa, xa, xa, modtab, g, wu, wg, cw, cb, wd)


def _rope_tables(ctx_len, rows, head_dim):
    row = jnp.repeat(jnp.arange(rows, dtype=F32), GRID_W)
    col = jnp.tile(jnp.arange(GRID_W, dtype=F32), rows)
    n_freq = head_dim // 4
    inv = ROPE_THETA ** (-jnp.arange(n_freq, dtype=F32) / n_freq)
    ang = jnp.concatenate([row[:, None] * inv, col[:, None] * inv], axis=-1)
    cos = jnp.concatenate([jnp.cos(ang), jnp.cos(ang)], axis=-1)
    sin = jnp.concatenate([-jnp.sin(ang), jnp.sin(ang)], axis=-1)
    reps = 256 // head_dim
    cos = jnp.tile(cos, (1, reps))
    sin = jnp.tile(sin, (1, reps))
    cos = jnp.concatenate([jnp.ones((ctx_len, 256), F32), cos], axis=0)
    sin = jnp.concatenate([jnp.zeros((ctx_len, 256), F32), sin], axis=0)
    return cos, sin


def _block_diag2(m0, m1):
    z01 = jnp.zeros((m0.shape[0], m1.shape[1]), m0.dtype)
    z10 = jnp.zeros((m1.shape[0], m0.shape[1]), m0.dtype)
    return jnp.concatenate([jnp.concatenate([m0, z01], axis=1), jnp.concatenate([z10, m1], axis=1)], axis=0)


def _gla_in_weights(w):
    d = w.shape[0]
    q, k, v, gf, gb, r = jnp.split(w, [128, 256, 512, 528, 544], axis=1)
    return jnp.concatenate([q, k, v, r, gf, gb, jnp.zeros((d, GLA_IN_PAD - GLA_IN), w.dtype)], axis=1)


def kernel(x, c, ctx, c_ctx, mod_w, mod_b, norm_mix_g, norm_ffn_g, w_in, w_out, rw_mu, rw_w0, rw_w2, rw_a0,
           rw_a2, rw_g2, rw_kk, rw_ka, rw_rk, rw_ln_g, rw_ln_b, da_qk_g, da_lam, da_subln_g, gla_a2, gla_ab,
           gla_norm_g, gqa_qk_g, ffn_w_up, ffn_conv_w, ffn_conv_b, ffn_w_down):
    b, seq, d = x.shape
    ctx_len = ctx.shape[1]
    depth = mod_w.shape[0]
    assert d == D_MODEL and seq % TILE == 0 and ctx_len % TILE == 0 and seq % GRID_W == 0
    nt_rows = ctx_len + seq
    nt = nt_rows // TILE
    nct = ctx_len // TILE
    ncc = ctx_len // CHUNK

    rows = -(-(b + 1) // 8) * 8
    cc = jnp.concatenate([c, c_ctx[None, :], jnp.zeros((rows - b - 1, d), F32)], axis=0)
    mods = _modulation(cc, mod_w, mod_b).reshape(depth, rows, 6, d)

    cda, sda = _rope_tables(ctx_len, seq // GRID_W, DA_HD)
    cgq, sgq = _rope_tables(ctx_len, seq // GRID_W, GQA_HD)

    xa = jnp.concatenate([ctx, x], axis=1)
    for i in range(depth):
        last = i == depth - 1
        t0 = nct if last else 0
        modtab = jnp.stack([jnp.broadcast_to(mods[i, b], (b, 6, d)), mods[i, :b]], axis=1)
        row = lambda v: v.reshape(1, -1)

        wi = w_in[i]
        wa = wi[:, 0:RW_IN].astype(BF16)
        wb = wi[:, RW_IN:RW_IN + DA_IN].astype(BF16)
        wc = _gla_in_weights(wi[:, RW_IN + DA_IN:RW_IN + DA_IN + GLA_IN]).astype(BF16)
        wd = wi[:, RW_IN + DA_IN + GLA_IN:].astype(BF16)
        a2pad = jnp.zeros((2, 128, GLA_KW), F32)
        a2pad = a2pad.at[0, 0:16].set(gla_a2[i, 0]).at[1, 16:32].set(gla_a2[i, 1])
        gains = (jnp.tile(da_qk_g[i], (1, 256 // DA_HD)), jnp.tile(gqa_qk_g[i], (1, 256 // GQA_HD)))
        pa, sr, daqt, dak, davt, gqqt, gqk, gqvt, *gla_terms = _in_projection(
            xa, modtab, row(norm_mix_g[i]), wa, wb, wc, wd, (cda, sda, cgq, sgq), gains, a2pad,
            gla_ab[i].reshape(2, 1, GLA_KW), nct)

        rw = dict(mu=rw_mu[i], w0=row(rw_w0[i]), w2=_block_diag2(rw_w2[i, 0], rw_w2[i, 1]),
                  a0=row(rw_a0[i]), a2=_block_diag2(rw_a2[i, 0], rw_a2[i, 1]), g2=rw_g2[i],
                  kk=row(rw_kk[i]), ka=row(rw_ka[i]), rk=row(rw_rk[i]))
        mrf, nyf, mrb, nyb, bonus, gate = _rwkv_chunks(pa, rw, nct)
        yf, yb, of, obk = _state_pass((mrf, nyf, mrb, nyb), gla_terms, ncc)

        lam_init = 0.8 - 0.6 * math.exp(-0.3 * i)
        ob = _attention(functools.partial(_da_heads, lam_init), daqt, dak, davt,
                        (da_lam[i], row(da_subln_g[i])), t0, nct)
        od = _attention(_gqa_heads, gqqt, gqk, gqvt, (), t0, nct)

        xa = _out_projection(xa, modtab, yf, yb, bonus, gate, row(rw_ln_g[i]), row(rw_ln_b[i]), ob, of, obk, sr,
                             row(jnp.tile(gla_norm_g[i], 4)), od, w_out[i].astype(BF16), t0, nct)
        wu = ffn_w_up[i]
        xa = _ffn(xa, modtab, row(norm_ffn_g[i]), wu[:, :D_FF].astype(BF16), wu[:, D_FF:].astype(BF16),
                  ffn_conv_w[i], row(ffn_conv_b[i]), ffn_w_down[i].astype(BF16), t0, nct, nt)
    return xa
```

```python
import functools
import math

import jax
import jax.numpy as jnp
from jax import lax
from jax.experimental import pallas as pl
from jax.experimental.pallas import tpu as pltpu

F32 = jnp.float32
BF16 = jnp.bfloat16
LOG2E = math.log2(math.e)

D_MODEL = 1024
GRID_W = 64
GROUP_W = 256
NORM_EPS = 1e-6
ROPE_THETA = 10000.0
RW_HD = 64
RW_IN = 1152
RW_LN_EPS = 64e-5
DA_HD = 32
DA_IN = 768
GLA_DK = 32
GLA_KW = 128
GLA_TAU = 16.0
GLA_IN = 800
GLA_IN_PAD = 896
GQA_HD = 64
GQA_KVW = 128
GQA_IN = 512
D_FF = 2816

CHUNK = 64
TILE = 256
HALO = 8
STATE_BATCH = 8
ATTN_GROUP = 2
VMEM_LIMIT = 56 * 1024 * 1024


def _cparams(*sem):
    return pltpu.CompilerParams(dimension_semantics=sem, vmem_limit_bytes=VMEM_LIMIT)


def _bdot(a, b):
    return jnp.dot(a.astype(BF16), b.astype(BF16), preferred_element_type=F32)


def _bdot_nt(a, b):
    return lax.dot_general(a.astype(BF16), b.astype(BF16), (((1,), (1,)), ((), ())),
                           preferred_element_type=F32)


def _bdot_tn(a, b):
    return jnp.dot(a.astype(F32).T.astype(BF16), b.astype(BF16), preferred_element_type=F32)


def _bf16_pieces(a):
    hi = a.astype(BF16)
    rest = a - hi.astype(F32)
    mid = rest.astype(BF16)
    lo = (rest - mid.astype(F32)).astype(BF16)
    return hi, mid, lo


def _sel_dot(a, sel):
    n = a.shape[0]
    out = jnp.dot(jnp.concatenate(_bf16_pieces(a), axis=0), sel.astype(BF16), preferred_element_type=F32)
    return out[0:n] + out[n:2 * n] + out[2 * n:]


def _dot_sel(sel, a):
    n = a.shape[1]
    out = jnp.dot(sel.astype(BF16), jnp.concatenate(_bf16_pieces(a), axis=1), preferred_element_type=F32)
    return out[:, 0:n] + out[:, n:2 * n] + out[:, 2 * n:]


def _iota(shape, dim):
    return lax.broadcasted_iota(jnp.int32, shape, dim)


def _div(x, n):
    return x >> (n.bit_length() - 1)


def _mod(x, n):
    return x & (n - 1)


def _sigmoid(x):
    return 1.0 / (1.0 + jnp.exp(-x))


def _log_sigmoid(x):
    return jnp.minimum(x, 0.0) - jnp.log(1.0 + jnp.exp(-jnp.abs(x)))


def _seg_matrix(width, seg, value):
    r = _div(_iota((width, width), 0), seg)
    c = _div(_iota((width, width), 1), seg)
    return jnp.where(r == c, value, 0.0).astype(F32)


def _modulate(x, g, shift, scale):
    y = x * lax.rsqrt(jnp.mean(x * x, axis=-1, keepdims=True) + NORM_EPS)
    return (y * g) * (1.0 + scale) + shift


def _shift_rows(x, hp_row, hn_row):
    n = x.shape[0]
    rows = _iota(x.shape, 0)
    prev = jnp.where(rows == 0, hp_row, pltpu.roll(x, 1, 0))
    nxt = jnp.where(rows == n - 1, hn_row, pltpu.roll(x, n - 1, 0))
    return prev, nxt


def _tile_type(i, nct):
    return jnp.where(i >= nct, 1, 0)


def _seq_edges(i, nct, nt):
    first = jnp.logical_or(i == 0, i == nct)
    last = jnp.logical_or(i == nct - 1, i == nt - 1)
    return first, last


def _mod_kernel(c_ref, w_ref, b_ref, o_ref):
    cc = c_ref[...]
    s = cc * _sigmoid(cc)
    o_ref[0] = _bdot(s, w_ref[0]) + b_ref[0]


def _modulation(cc, mod_w, mod_b):
    depth, d, n = mod_w.shape
    tn = 1536
    rows = cc.shape[0]
    return pl.pallas_call(
        _mod_kernel,
        grid=(depth, n // tn),
        in_specs=[pl.BlockSpec((rows, d), lambda l, j: (0, 0)),
                  pl.BlockSpec((1, d, tn), lambda l, j: (l, 0, j)),
                  pl.BlockSpec((1, 1, tn), lambda l, j: (l, 0, j))],
        out_specs=pl.BlockSpec((1, rows, tn), lambda l, j: (l, 0, j)),
        out_shape=jax.ShapeDtypeStruct((depth, rows, n), F32),
        compiler_params=_cparams("parallel", "parallel"),
    )(cc, mod_w, mod_b.reshape(depth, 1, n))


def _big(x, width_seg):
    t = jnp.concatenate([x, x, x, x], axis=0)
    rh = _div(_iota(t.shape, 0), CHUNK)
    ch = _div(_iota(t.shape, 1), width_seg)
    return jnp.where(rh == ch, t, 0.0)


def _collapse(x):
    return x[0:CHUNK] + x[CHUNK:2 * CHUNK] + x[2 * CHUNK:3 * CHUNK] + x[3 * CHUNK:4 * CHUNK]


def _chunk_masks(rev):
    n = 4 * CHUNK
    ri = _iota((CHUNK, CHUNK), 0)
    ci = _iota((CHUNK, CHUNK), 1)
    tri = jnp.where((ci >= ri) if rev else (ci <= ri), 1.0, 0.0).astype(F32)
    rt_row = _iota((CHUNK, n), 0)
    ct_row = _mod(_iota((CHUNK, n), 1), CHUNK)
    strict_row = (ct_row > rt_row) if rev else (ct_row < rt_row)
    incl_row = (ct_row >= rt_row) if rev else (ct_row <= rt_row)
    eye_row = ct_row == rt_row
    return tri, strict_row, incl_row, eye_row


def _lockstep(gens):
    results = [None] * len(gens)
    active = list(range(len(gens)))
    while active:
        for idx in list(active):
            try:
                next(gens[idx])
            except StopIteration as stop:
                results[idx] = stop.value
                active.remove(idx)
    return results


def _rw_chunk(rev, r, v, kk, lw, bv, kd):
    n = 4 * CHUNK
    tri, strict_row, incl_row, eye_row = _chunk_masks(rev)
    same_head = _div(_iota((n, n), 0), CHUNK) == _div(_iota((n, n), 1), CHUNK)
    c = _dot_sel(tri, lw)
    yield
    ctot = jnp.sum(lw, axis=0, keepdims=True)
    at = -kk * jnp.exp(c - lw)
    rt = r * jnp.exp(c)
    eneg = jnp.exp(-c)
    bt = bv * eneg
    kt = kd * eneg
    eend = jnp.exp(ctot - c)
    bh = bv * eend
    kh = kd * eend
    dend = jnp.exp(ctot)
    at_big = _big(at, RW_HD).astype(BF16)
    v_big = _big(v, RW_HD).astype(BF16)
    lhs = jnp.concatenate([at, rt], axis=0)
    gb = _bdot_nt(lhs, _big(bt, RW_HD))
    gk = _bdot_nt(lhs, _big(kt, RW_HD))
    yield
    nrow = jnp.where(strict_row, gb[0:CHUNK], 0.0)
    a_rb = jnp.where(incl_row, gb[CHUNK:], 0.0)
    a_ak = jnp.where(strict_row, gk[0:CHUNK], 0.0)
    a_rk = jnp.where(incl_row, gk[CHUNK:], 0.0)
    t_row = jnp.where(eye_row, 1.0, 0.0) + nrow
    q_row = _bdot(nrow, _big(nrow, CHUNK))
    wk = _bdot(jnp.concatenate([a_ak, a_rk], axis=0), v_big)
    yield
    for level in range(5):
        q_big = _big(q_row, CHUNK).astype(BF16)
        if level < 4:
            res = _bdot(jnp.concatenate([t_row, q_row], axis=0), q_big)
            t_row = t_row + res[0:CHUNK]
            q_row = res[CHUNK:]
        else:
            t_row = t_row + _bdot(t_row, q_big)
        yield
    ah = _bdot(t_row, at_big)
    uv = _bdot(t_row, _big(wk[0:CHUNK], RW_HD))
    yield
    rh = rt + _bdot(a_rb, _big(ah, RW_HD))
    yh = _bdot(a_rb, _big(uv, RW_HD)) + wk[CHUNK:]
    m_row = jnp.where(eye_row, dend, 0.0) + _collapse(jnp.where(same_head, _bdot_tn(bh, ah), 0.0))
    n_row = _collapse(jnp.where(same_head, _bdot_tn(bh, uv) + _bdot_tn(kh, v), 0.0))
    return (jnp.concatenate([m_row, rh], axis=0).astype(BF16), jnp.concatenate([n_row, yh], axis=0))


def _rw_chunk_kernel(nct, nt, pa_ref, hp_ref, hn_ref, mu_ref, w0_ref, w2_ref, a0_ref, a2_ref,
                     g2_ref, kk_ref, ka_ref, rk_ref, mrf, nyf, mrb, nyb, bonus_out, gate_out):
    i = pl.program_id(1)
    first, last = _seq_edges(i, nct, nt)
    x = pa_ref[0]
    hp = jnp.where(first, 0.0, hp_ref[0, HALO - 1:HALO, :])
    hn = jnp.where(last, 0.0, hn_ref[0, 0:1, :])
    prev, nxt = _shift_rows(x, hp, hn)
    mu = mu_ref[...]
    xm = x + mu[0:1] * (prev - x) + mu[1:2] * (nxt - x)
    r = xm[:, 0:256]
    k = xm[:, 256:512]
    v = xm[:, 512:768]
    wfb = xm[:, 768:896]
    afb = xm[:, 896:1024]
    g = xm[:, 1024:1152]
    seg = _seg_matrix(GROUP_W, RW_HD, 1.0)
    gate_out[0] = _bdot(_sigmoid(g), g2_ref[...])
    kk = k * kk_ref[...]
    kk = kk * lax.rsqrt(_sel_dot(kk * kk, seg) + 1e-12)
    wl = w0_ref[...] + _bdot(jnp.tanh(wfb), w2_ref[...])
    lw = -jnp.exp(_log_sigmoid(wl) - 0.5)
    a = _sigmoid(a0_ref[...] + _bdot(afb, a2_ref[...]))
    ka = ka_ref[...]
    kd = [k * (1.0 + (a[:, 256 * d:256 * d + 256] - 1.0) * ka) for d in range(2)]
    bv = [kk * a[:, 256 * d:256 * d + 256] for d in range(2)]
    rk = rk_ref[...]
    bonus_out[0] = _sel_dot(r * (kd[0] + kd[1]) * rk, seg) * v
    units = [(d, ck) for ck in range(TILE // CHUNK) for d in range(2)]
    gens = []
    for d, ck in units:
        rows = slice(ck * CHUNK, (ck + 1) * CHUNK)
        gens.append(_rw_chunk(d == 1, r[rows], v[rows], kk[rows], lw[rows, 256 * d:256 * d + 256],
                              bv[d][rows], kd[d][rows]))
    for (d, ck), (mr, ny) in zip(units, _lockstep(gens)):
        mr_out, ny_out = ((mrf, nyf), (mrb, nyb))[d]
        mr_out[0, 2 * ck * CHUNK:2 * (ck + 1) * CHUNK, :] = mr
        ny_out[0, 2 * ck * CHUNK:2 * (ck + 1) * CHUNK, :] = ny


def _rwkv_chunks(pa, rw, nct):
    b, nt_rows, _ = pa.shape
    nt = nt_rows // TILE
    hb = TILE // HALO
    nhb = nt_rows // HALO
    full = lambda w: pl.BlockSpec(w.shape, lambda bi, i: (0,) * w.ndim)
    tile = lambda rows, w: pl.BlockSpec((1, rows, w), lambda bi, i: (bi, i, 0))
    consts = (rw["mu"], rw["w0"], rw["w2"], rw["a0"], rw["a2"], rw["g2"], rw["kk"], rw["ka"], rw["rk"])
    pair = lambda dt: jax.ShapeDtypeStruct((b, 2 * nt_rows, 256), dt)
    return pl.pallas_call(
        functools.partial(_rw_chunk_kernel, nct, nt),
        grid=(b, nt),
        in_specs=[tile(TILE, RW_IN),
                  pl.BlockSpec((1, HALO, RW_IN), lambda bi, i: (bi, jnp.maximum(i * hb - 1, 0), 0)),
                  pl.BlockSpec((1, HALO, RW_IN), lambda bi, i: (bi, jnp.minimum(i * hb + hb, nhb - 1), 0))]
                 + [full(w) for w in consts],
        out_specs=[tile(2 * TILE, 256)] * 4 + [tile(TILE, 256)] * 2,
        out_shape=[pair(BF16), pair(F32), pair(BF16), pair(F32),
                   jax.ShapeDtypeStruct((b, nt_rows, 256), F32), jax.ShapeDtypeStruct((b, nt_rows, 256), F32)],
        compiler_params=_cparams("parallel", "parallel"),
    )(pa, pa, pa, *consts)


def _gla_chunk(rev, pc, a2, ab):
    n = 4 * CHUNK
    tri, _, incl_row, _ = _chunk_masks(rev)
    q = pc[:, 0:128] * (GLA_DK ** -0.5)
    k = pc[:, 128:256]
    v = pc[:, 256:512]
    lg = _log_sigmoid(_bdot(pc[:, 768:896], a2) + ab) / GLA_TAU
    yield
    bcum = _dot_sel(tri, lg)
    yield
    btot = jnp.sum(lg, axis=0, keepdims=True)
    q_in = q * jnp.exp(bcum)
    k_in = k * jnp.exp(-bcum)
    k_end = k * jnp.exp(btot - bcum)
    dec = jnp.exp(btot)
    att = jnp.where(incl_row, _bdot_nt(q_in, _big(k_in, GLA_DK)), 0.0)
    same_head = _div(_iota((n, GLA_KW), 0), 64) == _div(_iota((n, GLA_KW), 1), GLA_DK)
    inc = _collapse(jnp.where(same_head, _bdot_tn(v, k_end), 0.0))
    yield
    o_intra = _bdot(att, _big(v, 64))
    return q_in.astype(BF16), o_intra, inc, jnp.broadcast_to(dec, (8, GLA_KW))


def _state_kernel(bg, mrf, nyf, mrb, nyb, qf, oif, gnf, dcf, qb, oib, gnb, dcb, yf, yb, of, ob, srw, sgl):
    @pl.when(pl.program_id(1) == 0)
    def _():
        srw[...] = jnp.zeros_like(srw)
        sgl[...] = jnp.zeros_like(sgl)

    for d, (mr, ny, y_out) in enumerate(((mrf, nyf, yf), (mrb, nyb, yb))):
        for j in range(bg):
            s_big = _big(srw[d, j], RW_HD).astype(BF16)
            res = jnp.dot(mr[j], s_big, preferred_element_type=F32) + ny[j]
            srw[d, j] = res[0:CHUNK]
            y_out[j] = res[CHUNK:]
    for d, (q, oi, gn, dc, o_out) in enumerate(((qf, oif, gnf, dcf, of), (qb, oib, gnb, dcb, ob))):
        for j in range(bg):
            st = sgl[d, j]
            o_out[j] = oi[j] + _bdot_nt(q[j], _big(st, GLA_DK))
            sgl[d, j] = st * dc[j, 0:1, :] + gn[j]


def _scan_order(ncc, nc):
    fwd = lambda s: s
    bwd = lambda s: jnp.where(s < ncc, ncc - 1 - s, nc - 1 - (s - ncc))
    return fwd, bwd


def _state_pass(rw_terms, gla_terms, ncc):
    mrf, nyf, mrb, nyb = rw_terms
    b = mrf.shape[0]
    bg = math.gcd(b, STATE_BATCH)
    nt_rows = mrf.shape[1] // 2
    nc = nt_rows // CHUNK
    fwd, bwd = _scan_order(ncc, nc)
    blk = lambda rows, w, order: pl.BlockSpec((bg, rows, w), lambda bi, s: (bi, order(s), 0))
    rw_in = lambda order: [blk(2 * CHUNK, 256, order)] * 2
    gla_in = lambda order: [blk(CHUNK, GLA_KW, order), blk(CHUNK, 256, order), blk(CHUNK, GLA_KW, order),
                            blk(8, GLA_KW, order)]
    out = lambda order: blk(CHUNK, 256, order)
    return pl.pallas_call(
        functools.partial(_state_kernel, bg),
        grid=(b // bg, nc),
        in_specs=rw_in(fwd) + rw_in(bwd) + gla_in(fwd) + gla_in(bwd),
        out_specs=[out(fwd), out(bwd), out(fwd), out(bwd)],
        out_shape=[jax.ShapeDtypeStruct((b, nt_rows, 256), F32)] * 4,
        scratch_shapes=[pltpu.VMEM((2, bg, CHUNK, 256), F32), pltpu.VMEM((2, bg, CHUNK, GLA_KW), F32)],
        compiler_params=_cparams("parallel", "arbitrary"),
    )(mrf, nyf, mrb, nyb, *gla_terms)


def _norm_rope(x, seg, gain, cos, sin, half):
    w = x.shape[1]
    xn = x * lax.rsqrt(_sel_dot(x * x, _seg_matrix(w, seg, 1.0 / seg)) + NORM_EPS) * gain
    lane = _mod(_iota(x.shape, 1), 2 * half)
    partner = jnp.where(lane < half, pltpu.roll(xn, w - half, 1), pltpu.roll(xn, half, 1))
    return xn * cos + partner * sin


def _qk_outputs(pb, pd, cda, sda, cgq, sgq, gda, ggq, daqt, dak, davt, gqqt, gqk, gqvt):
    q = _norm_rope(pb[:, 0:256], DA_HD, gda[0:1], cda, sda, DA_HD // 2)
    daqt[0] = (q * (DA_HD ** -0.5 * LOG2E)).T.astype(BF16)
    dak[0] = _norm_rope(pb[:, 256:512], DA_HD, gda[1:2], cda, sda, DA_HD // 2).astype(BF16)
    davt[0] = pb[:, 512:768].T.astype(BF16)
    q = _norm_rope(pd[:, 0:256], GQA_HD, ggq[0:1], cgq, sgq, GQA_HD // 2)
    gqqt[0] = (q * (GQA_HD ** -0.5 * LOG2E)).T.astype(BF16)
    gqk[0] = _norm_rope(pd[:, 256:384], GQA_HD, ggq[1:2, 0:128], cgq[:, 0:128], sgq[:, 0:128],
                        GQA_HD // 2).astype(BF16)
    gqvt[0] = pd[:, 384:512].T.astype(BF16)


def _inproj_kernel(x_ref, mod_ref, g_ref, wa, wb, wc, wd, cda, sda, cgq, sgq, gda, ggq, a2_ref, ab_ref,
                   pa, sr, daqt, dak, davt, gqqt, gqk, gqvt, qf, of, nf, df, qb, ob, nb, db):
    m = mod_ref[0, 0]
    h = _modulate(x_ref[0], g_ref[...], m[0:1], m[1:2]).astype(BF16)
    pb = jnp.dot(h, wb[...], preferred_element_type=F32)
    pd = jnp.dot(h, wd[...], preferred_element_type=F32)
    pc = jnp.dot(h, wc[...], preferred_element_type=F32)
    _qk_outputs(pb, pd, cda[...], sda[...], cgq[...], sgq[...], gda[...], ggq[...],
                daqt, dak, davt, gqqt, gqk, gqvt)
    pa[0] = jnp.dot(h, wa[...], preferred_element_type=F32)
    r = pc[:, 512:768]
    sr[0] = r * _sigmoid(r)
    units = [(d, ck) for ck in range(TILE // CHUNK) for d in range(2)]
    gens = [_gla_chunk(d == 1, pc[ck * CHUNK:(ck + 1) * CHUNK], a2_ref[d], ab_ref[d]) for d, ck in units]
    for (d, ck), res in zip(units, _lockstep(gens)):
        outs = ((qf, of, nf, df), (qb, ob, nb, db))[d]
        for ref, val in zip(outs[:3], res[:3]):
            ref[0, ck * CHUNK:(ck + 1) * CHUNK, :] = val
        outs[3][0, 8 * ck:8 * ck + 8, :] = res[3]


def _in_projection(xa, modtab, g, wa, wb, wc, wd, tables, gains, a2pad, ab, nct):
    b, nt_rows, d = xa.shape
    nt = nt_rows // TILE
    full = lambda w: pl.BlockSpec(w.shape, lambda bi, i: (0,) * w.ndim)
    tile = lambda rows, w: pl.BlockSpec((1, rows, w), lambda bi, i: (bi, i, 0))
    tab = pl.BlockSpec((TILE, 256), lambda bi, i: (i, 0))
    tr = lambda w: pl.BlockSpec((1, w, TILE), lambda bi, i: (bi, 0, i))
    f32 = lambda rows, w: jax.ShapeDtypeStruct((b, rows, w), F32)
    rowmajor = lambda w: jax.ShapeDtypeStruct((b, nt_rows, w), BF16)
    transposed = lambda w: jax.ShapeDtypeStruct((b, w, nt_rows), BF16)
    gla_specs = [tile(TILE, GLA_KW), tile(TILE, 256), tile(TILE, GLA_KW), tile(TILE // CHUNK * 8, GLA_KW)]
    gla_shapes = [rowmajor(GLA_KW), f32(nt_rows, 256), f32(nt_rows, GLA_KW), f32(nt_rows // CHUNK * 8, GLA_KW)]
    return pl.pallas_call(
        _inproj_kernel,
        grid=(b, nt),
        in_specs=[tile(TILE, d),
                  pl.BlockSpec((1, 1, 6, d), lambda bi, i: (bi, _tile_type(i, nct), 0, 0)),
                  pl.BlockSpec((1, d), lambda bi, i: (0, 0)),
                  full(wa), full(wb), full(wc), full(wd), tab, tab, tab, tab,
                  full(gains[0]), full(gains[1]), full(a2pad), full(ab)],
        out_specs=[tile(TILE, RW_IN), tile(TILE, 256),
                   tr(256), tile(TILE, 256), tr(256), tr(256), tile(TILE, GQA_KVW), tr(GQA_KVW)]
                  + gla_specs + gla_specs,
        out_shape=[f32(nt_rows, RW_IN), f32(nt_rows, 256),
                   transposed(256), rowmajor(256), transposed(256),
                   transposed(256), rowmajor(GQA_KVW), transposed(GQA_KVW)]
                  + gla_shapes + gla_shapes,
        compiler_params=_cparams("parallel", "parallel"),
    )(xa, modtab, g, wa, wb, wc, wd, *tables, *gains, a2pad, ab)


def _staggered(gens):
    results = [None] * len(gens)
    groups = [list(range(g, min(g + ATTN_GROUP, len(gens)))) for g in range(0, len(gens), ATTN_GROUP)]
    for t in range(len(groups) + 2):
        for lag in range(2):
            if 0 <= t - lag < len(groups):
                for u in groups[t - lag]:
                    next(gens[u])
        if 0 <= t - 2 < len(groups):
            for u in groups[t - 2]:
                try:
                    next(gens[u])
                except StopIteration as stop:
                    results[u] = stop.value
    return results


def _softmax_pv_t(load_k, load_wq, load_vt):
    s = jnp.dot(load_k(), load_wq(), preferred_element_type=F32)
    yield
    e = jnp.exp2(s - jnp.max(s, axis=0, keepdims=True))
    l = jnp.sum(e, axis=0, keepdims=True)
    e = e.astype(BF16)
    yield
    return jnp.dot(load_vt(), e, preferred_element_type=F32) / l


def _feature_rows(qt_ref, lo, width):
    qt = qt_ref[0]
    rows = _iota(qt.shape, 0)
    keep = jnp.logical_and(rows >= lo, rows < lo + width)
    return jnp.where(keep, qt, jnp.zeros_like(qt))


def _da_heads(lam_init, qt_ref, k_ref, vt_ref, lam_ref, g_ref, nk):
    lv = lam_ref[...]
    lam = (jnp.exp(jnp.sum(lv[0:1] * lv[1:2], axis=1, keepdims=True))
           - jnp.exp(jnp.sum(lv[2:3] * lv[3:4], axis=1, keepdims=True)) + lam_init)
    gens = []
    for h in range(4):
        for m in range(2):
            lo = 64 * h + 32 * m
            gens.append(_softmax_pv_t(lambda: k_ref[0, 0:nk, :],
                                      lambda lo=lo: _feature_rows(qt_ref, lo, DA_HD),
                                      lambda h=h: vt_ref[0, 64 * h:64 * h + 64, 0:nk]))
    o = _staggered(gens)
    outs = []
    for h in range(4):
        oh = (o[2 * h] - lam * o[2 * h + 1]).T
        oh = oh * lax.rsqrt(jnp.mean(oh * oh, axis=-1, keepdims=True) + NORM_EPS) * g_ref[...]
        outs.append(oh * (1.0 - lam_init))
    return jnp.concatenate(outs, axis=1)


def _gqa_heads(qt_ref, k_ref, vt_ref, nk):
    def wq(h):
        qh = qt_ref[0, 64 * h:64 * h + 64, :]
        zero = jnp.zeros_like(qh)
        return jnp.concatenate([qh, zero] if h // 2 == 0 else [zero, qh], axis=0)

    gens = [_softmax_pv_t(lambda: k_ref[0, 0:nk, :], lambda h=h: wq(h),
                          lambda g=h // 2: vt_ref[0, 64 * g:64 * g + 64, 0:nk]) for h in range(4)]
    return jnp.concatenate([o.T for o in _staggered(gens)], axis=1)


def _attn_kernel(heads_fn, q_tile0, nct, ctx_len, nk_all, qt_ref, k_ref, vt_ref, *rest):
    *extra, o_ref = rest
    i = pl.program_id(1) + q_tile0

    @pl.when(i < nct)
    def _():
        o_ref[0] = heads_fn(qt_ref, k_ref, vt_ref, *extra, ctx_len)

    @pl.when(i >= nct)
    def _():
        o_ref[0] = heads_fn(qt_ref, k_ref, vt_ref, *extra, nk_all)


def _attention(heads_fn, qt, k, vt, extra, q_tile0, nct):
    b, _, nt_rows = qt.shape
    nqt = nt_rows // TILE - q_tile0
    full = lambda w: pl.BlockSpec(w.shape, lambda bi, i: (0,) * w.ndim)
    return pl.pallas_call(
        functools.partial(_attn_kernel, heads_fn, q_tile0, nct, nct * TILE, nt_rows),
        grid=(b, nqt),
        in_specs=[pl.BlockSpec((1, 256, TILE), lambda bi, i: (bi, 0, i + q_tile0)),
                  pl.BlockSpec((1,) + k.shape[1:], lambda bi, i: (bi, 0, 0)),
                  pl.BlockSpec((1,) + vt.shape[1:], lambda bi, i: (bi, 0, 0))]
                 + [full(w) for w in extra],
        out_specs=pl.BlockSpec((1, TILE, 256), lambda bi, i: (bi, i, 0)),
        out_shape=jax.ShapeDtypeStruct((b, nqt * TILE, 256), F32),
        compiler_params=_cparams("parallel", "parallel"),
    )(qt, k, vt, *extra)


MIX_INPUTS = 10


def _mix_ffn_kernel(t0, nct, nt, *refs):
    triples = [refs[3 * j:3 * j + 3] for j in range(MIX_INPUTS)]
    mod_ref, lng, lnb, glag, gffn, wo_ref, wu_ref, wg_ref, cw_ref, cb_ref, wd_ref, o_ref = refs[3 * MIX_INPUTS:]
    ext = lambda t: jnp.concatenate([t[1][0], t[0][0], t[2][0]], axis=0)
    x, yf, yb, bonus, gate, ob, of, obk, sr, od = triples
    i = pl.program_id(1) + t0
    first, last = _seq_edges(i, nct, nt)
    m = mod_ref[0, 0]
    seg = _seg_matrix(GROUP_W, 64, 1.0 / 64)
    y = ext(yf) + ext(yb)
    mean = _sel_dot(y, seg)
    yc = y - mean
    var = _sel_dot(yc * yc, seg)
    oa = ((yc * lax.rsqrt(var + RW_LN_EPS)) * lng[...] + lnb[...] + ext(bonus)) * ext(gate)
    o = ext(of) + ext(obk)
    oc = (o * lax.rsqrt(_sel_dot(o * o, seg) + NORM_EPS) * glag[...]) * ext(sr)
    w = wo_ref[...]
    mix = (_bdot(oa, w[0:256]) + _bdot(ext(ob), w[256:512]) + _bdot(oc, w[512:768]) + _bdot(ext(od), w[768:1024]))
    xn = ext(x) + m[2:3] * mix
    h = _modulate(xn, gffn[...], m[3:4], m[4:5])
    rows = _iota(h.shape, 0)
    h = jnp.where(rows < HALO, jnp.where(first, 0.0, h), h)
    h = jnp.where(rows >= HALO + TILE, jnp.where(last, 0.0, h), h)
    gext = jnp.dot(h.astype(BF16), wg_ref[...], preferred_element_type=F32)
    n = TILE + 2 * HALO
    prev = pltpu.roll(gext, 1, 0)[HALO:HALO + TILE]
    gg = gext[HALO:HALO + TILE]
    nxt = pltpu.roll(gext, n - 1, 0)[HALO:HALO + TILE]
    u = jnp.dot(h[HALO:HALO + TILE].astype(BF16), wu_ref[...], preferred_element_type=F32)
    cw = cw_ref[...]
    gc = cw[0:1] * prev + cw[1:2] * gg + cw[2:3] * nxt + cb_ref[...]
    act = (gc * _sigmoid(gc)) * u
    o_ref[0] = xn[HALO:HALO + TILE] + m[5:6] * _bdot(act, wd_ref[...])


def _mix_ffn(xa, modtab, full_len, local, rows_, gffn, wo, wu, wg, cw, cb, wd, t0, nct):
    x, yf, yb, bonus, gate, of, obk, sr = full_len
    ob, od = local
    b, nt_rows, d = xa.shape
    nt = nt_rows // TILE
    n = nt - t0
    hb = TILE // HALO

    def triple(arr, off):
        w_ = arr.shape[2]
        nhb = arr.shape[1] // HALO
        return [pl.BlockSpec((1, TILE, w_), lambda bi, i: (bi, i + off, 0)),
                pl.BlockSpec((1, HALO, w_), lambda bi, i: (bi, jnp.maximum((i + off) * hb - 1, 0), 0)),
                pl.BlockSpec((1, HALO, w_), lambda bi, i: (bi, jnp.minimum((i + off) * hb + hb, nhb - 1), 0))]

    ordered = [(x, t0), (yf, t0), (yb, t0), (bonus, t0), (gate, t0), (ob, 0), (of, t0), (obk, t0), (sr, t0), (od, 0)]
    assert len(ordered) == MIX_INPUTS
    in_specs, args = [], []
    for arr, off in ordered:
        in_specs += triple(arr, off)
        args += [arr, arr, arr]
    row = lambda w_: pl.BlockSpec((1, w_), lambda bi, i: (0, 0))
    const = lambda w_: pl.BlockSpec(w_.shape, lambda bi, i: (0, 0), pipeline_mode=pl.Buffered(1))
    in_specs += [pl.BlockSpec((1, 1, 6, d), lambda bi, i: (bi, _tile_type(i + t0, nct), 0, 0)),
                 row(256), row(256), row(256), row(d), const(wo), const(wu), const(wg),
                 pl.BlockSpec(cw.shape, lambda bi, i: (0, 0)), pl.BlockSpec(cb.shape, lambda bi, i: (0, 0)),
                 const(wd)]
    return pl.pallas_call(
        functools.partial(_mix_ffn_kernel, t0, nct, nt),
        grid=(b, n),
        in_specs=in_specs,
        out_specs=pl.BlockSpec((1, TILE, d), lambda bi, i: (bi, i, 0)),
        out_shape=jax.ShapeDtypeStruct((b, n * TILE, d), F32),
        compiler_params=_cparams("parallel", "parallel"),
    )(*args, modtab, *rows_, gffn, wo, wu, wg, cw, cb, wd)


def _rope_tables(ctx_len, rows, head_dim):
    row = jnp.repeat(jnp.arange(rows, dtype=F32), GRID_W)
    col = jnp.tile(jnp.arange(GRID_W, dtype=F32), rows)
    n_freq = head_dim // 4
    inv = ROPE_THETA ** (-jnp.arange(n_freq, dtype=F32) / n_freq)
    ang = jnp.concatenate([row[:, None] * inv, col[:, None] * inv], axis=-1)
    cos = jnp.concatenate([jnp.cos(ang), jnp.cos(ang)], axis=-1)
    sin = jnp.concatenate([-jnp.sin(ang), jnp.sin(ang)], axis=-1)
    reps = 256 // head_dim
    cos = jnp.tile(cos, (1, reps))
    sin = jnp.tile(sin, (1, reps))
    cos = jnp.concatenate([jnp.ones((ctx_len, 256), F32), cos], axis=0)
    sin = jnp.concatenate([jnp.zeros((ctx_len, 256), F32), sin], axis=0)
    return cos, sin


def _block_diag2(m0, m1):
    z01 = jnp.zeros((m0.shape[0], m1.shape[1]), m0.dtype)
    z10 = jnp.zeros((m1.shape[0], m0.shape[1]), m0.dtype)
    return jnp.concatenate([jnp.concatenate([m0, z01], axis=1), jnp.concatenate([z10, m1], axis=1)], axis=0)


def _gla_in_weights(w):
    d = w.shape[0]
    q, k, v, gf, gb, r = jnp.split(w, [128, 256, 512, 528, 544], axis=1)
    return jnp.concatenate([q, k, v, r, gf, gb, jnp.zeros((d, GLA_IN_PAD - GLA_IN), w.dtype)], axis=1)


def kernel(x, c, ctx, c_ctx, mod_w, mod_b, norm_mix_g, norm_ffn_g, w_in, w_out, rw_mu, rw_w0, rw_w2, rw_a0,
           rw_a2, rw_g2, rw_kk, rw_ka, rw_rk, rw_ln_g, rw_ln_b, da_qk_g, da_lam, da_subln_g, gla_a2, gla_ab,
           gla_norm_g, gqa_qk_g, ffn_w_up, ffn_conv_w, ffn_conv_b, ffn_w_down):
    b, seq, d = x.shape
    ctx_len = ctx.shape[1]
    depth = mod_w.shape[0]
    assert d == D_MODEL and seq % TILE == 0 and ctx_len % TILE == 0 and seq % GRID_W == 0
    nt_rows = ctx_len + seq
    nt = nt_rows // TILE
    nct = ctx_len // TILE
    ncc = ctx_len // CHUNK

    rows = -(-(b + 1) // 8) * 8
    cc = jnp.concatenate([c, c_ctx[None, :], jnp.zeros((rows - b - 1, d), F32)], axis=0)
    mods = _modulation(cc, mod_w, mod_b).reshape(depth, rows, 6, d)

    cda, sda = _rope_tables(ctx_len, seq // GRID_W, DA_HD)
    cgq, sgq = _rope_tables(ctx_len, seq // GRID_W, GQA_HD)

    xa = jnp.concatenate([ctx, x], axis=1)
    for i in range(depth):
        last = i == depth - 1
        t0 = nct if last else 0
        modtab = jnp.stack([jnp.broadcast_to(mods[i, b], (b, 6, d)), mods[i, :b]], axis=1)
        row = lambda v: v.reshape(1, -1)

        wi = w_in[i]
        wa = wi[:, 0:RW_IN].astype(BF16)
        wb = wi[:, RW_IN:RW_IN + DA_IN].astype(BF16)
        wc = _gla_in_weights(wi[:, RW_IN + DA_IN:RW_IN + DA_IN + GLA_IN]).astype(BF16)
        wd = wi[:, RW_IN + DA_IN + GLA_IN:].astype(BF16)
        a2pad = jnp.zeros((2, 128, GLA_KW), F32)
        a2pad = a2pad.at[0, 0:16].set(gla_a2[i, 0]).at[1, 16:32].set(gla_a2[i, 1])
        gains = (jnp.tile(da_qk_g[i], (1, 256 // DA_HD)), jnp.tile(gqa_qk_g[i], (1, 256 // GQA_HD)))
        pa, sr, daqt, dak, davt, gqqt, gqk, gqvt, *gla_terms = _in_projection(
            xa, modtab, row(norm_mix_g[i]), wa, wb, wc, wd, (cda, sda, cgq, sgq), gains, a2pad,
            gla_ab[i].reshape(2, 1, GLA_KW), nct)

        rw = dict(mu=rw_mu[i], w0=row(rw_w0[i]), w2=_block_diag2(rw_w2[i, 0], rw_w2[i, 1]),
                  a0=row(rw_a0[i]), a2=_block_diag2(rw_a2[i, 0], rw_a2[i, 1]), g2=rw_g2[i],
                  kk=row(rw_kk[i]), ka=row(rw_ka[i]), rk=row(rw_rk[i]))
        mrf, nyf, mrb, nyb, bonus, gate = _rwkv_chunks(pa, rw, nct)
        yf, yb, of, obk = _state_pass((mrf, nyf, mrb, nyb), gla_terms, ncc)

        lam_init = 0.8 - 0.6 * math.exp(-0.3 * i)
        ob = _attention(functools.partial(_da_heads, lam_init), daqt, dak, davt,
                        (da_lam[i], row(da_subln_g[i])), t0, nct)
        od = _attention(_gqa_heads, gqqt, gqk, gqvt, (), t0, nct)

        wu = ffn_w_up[i]
        xa = _mix_ffn(xa, modtab, (xa, yf, yb, bonus, gate, of, obk, sr), (ob, od),
                      (row(rw_ln_g[i]), row(rw_ln_b[i]), row(jnp.tile(gla_norm_g[i], 4))), row(norm_ffn_g[i]),
                      w_out[i].astype(BF16), wu[:, :D_FF].astype(BF16), wu[:, D_FF:].astype(BF16),
                      ffn_conv_w[i], row(ffn_conv_b[i]), ffn_w_down[i].astype(BF16), t0, nct)
    return xa
```
